```python
import jax, jax.numpy as jnp
from jax import lax
import numpy as np

D_MODEL = 1024
BATCH = 2
SEQ = 8192
DEPTH = 2

N_A_LAYERS = DEPTH // 2
N_B_LAYERS = DEPTH - N_A_LAYERS

SSM_EXPAND = 2
D_INNER = SSM_EXPAND * D_MODEL
SSM_HEAD_DIM = 64
SSM_HEADS = D_INNER // SSM_HEAD_DIM
SSM_GROUPS = 4
SSM_STATE = 128
CONV_WIDTH = 4
CHUNK = 256
CONV_DIM = D_INNER + 2 * SSM_GROUPS * SSM_STATE
SSM_IN_DIM = D_INNER + CONV_DIM + SSM_HEADS

MLA_HEADS = 16
QK_NOPE = 64
QK_ROPE = 32
V_HEAD = 64
Q_LORA = 384
KV_LORA = 256
ROPE_BASE = 10000.0
Q_BLOCK = 128
MLA_IN_DIM = Q_LORA + MLA_HEADS * V_HEAD

EPS = 1e-6

kernel_name = "yoco_mamba2_mla_hybrid"


def rms_norm(x, g):
    xf = x.astype(jnp.float32)
    y = xf * lax.rsqrt(jnp.mean(xf * xf, axis=-1, keepdims=True) + EPS)
    return (y * g.astype(jnp.float32)).astype(x.dtype)


def rope_tables(positions):
    inv = ROPE_BASE ** (-jnp.arange(0, QK_ROPE, 2, dtype=jnp.float32) / QK_ROPE)
    ang = positions.astype(jnp.float32)[..., None] * inv
    return jnp.cos(ang), jnp.sin(ang)


def apply_rope(x, cos, sin):
    x1, x2 = jnp.split(x.astype(jnp.float32), 2, axis=-1)
    out = jnp.concatenate([x1 * cos - x2 * sin, x1 * sin + x2 * cos], axis=-1)
    return out.astype(x.dtype)


def causal_depthwise_conv(u, w, b):
    K = w.shape[0]
    S = u.shape[1]
    up = jnp.pad(u, ((0, 0), (K - 1, 0), (0, 0)))
    out = b
    for k in range(K):
        out = out + up[:, k:k + S, :] * w[k]
    return out


def ssd_chunked_scan(xh, dt, A, Bm, Cm):
    Bsz, S, H, P = xh.shape
    G, N = Bm.shape[2], Bm.shape[3]
    Hg = H // G
    nc = -(-S // CHUNK)
    pad = nc * CHUNK - S

    def chunkify(t):
        t = jnp.pad(t, [(0, 0), (0, pad)] + [(0, 0)] * (t.ndim - 2))
        t = t.reshape((Bsz, nc, CHUNK) + t.shape[2:])
        return jnp.moveaxis(t, 1, 0)

    xc = chunkify(xh.reshape(Bsz, S, G, Hg, P))
    dtc = chunkify(dt.reshape(Bsz, S, G, Hg))
    Bc = chunkify(Bm)
    Cc = chunkify(Cm)
    A_g = A.reshape(G, Hg)
    causal = jnp.tril(jnp.ones((CHUNK, CHUNK), dtype=bool))[None, :, :, None, None]

    def step(state, inp):
        x_c, dt_c, B_c, C_c = inp
        cum = jnp.cumsum(dt_c * A_g, axis=1)
        seg = cum[:, :, None] - cum[:, None, :]
        L = jnp.exp(jnp.where(causal, seg, -jnp.inf))
        CB = jnp.einsum('btgn,bsgn->btsg', C_c, B_c)
        y_intra = jnp.einsum('btsg,btsgh,bsgh,bsghp->btghp', CB, L, dt_c, x_c)
        y_inter = jnp.einsum('btgn,bghpn,btgh->btghp', C_c, state, jnp.exp(cum))
        w_end = jnp.exp(cum[:, -1:] - cum) * dt_c
        new_state = state * jnp.exp(cum[:, -1])[..., None, None] + \
            jnp.einsum('bsgn,bsgh,bsghp->bghpn', B_c, w_end, x_c)
        return new_state, y_intra + y_inter

    state0 = jnp.zeros((Bsz, G, Hg, P, N), jnp.float32)
    _, yc = lax.scan(step, state0, (xc, dtc, Bc, Cc))
    return jnp.moveaxis(yc, 0, 1).reshape(Bsz, nc * CHUNK, H, P)[:, :S]


def mamba2_mixer(h, w_in, conv_w, conv_b, dt_bias, A_log, D_skip, g_out, w_out):
    Bsz, S, _ = h.shape
    proj = h @ w_in
    z, xBC, dt_raw = jnp.split(proj, [D_INNER, D_INNER + CONV_DIM], axis=-1)
    xBC = jax.nn.silu(causal_depthwise_conv(xBC, conv_w, conv_b))
    xs, Bm, Cm = jnp.split(xBC, [D_INNER, D_INNER + SSM_GROUPS * SSM_STATE], axis=-1)
    dt = jax.nn.softplus((dt_raw + dt_bias).astype(jnp.float32))
    A = -jnp.exp(A_log.astype(jnp.float32))
    xh = xs.reshape(Bsz, S, SSM_HEADS, SSM_HEAD_DIM).astype(jnp.float32)
    y = ssd_chunked_scan(
        xh, dt, A,
        Bm.reshape(Bsz, S, SSM_GROUPS, SSM_STATE).astype(jnp.float32),
        Cm.reshape(Bsz, S, SSM_GROUPS, SSM_STATE).astype(jnp.float32))
    y = y + D_skip.astype(jnp.float32)[:, None] * xh
    y = y.reshape(Bsz, S, D_INNER) * jax.nn.silu(z.astype(jnp.float32))
    y = y.reshape(Bsz, S, SSM_GROUPS, D_INNER // SSM_GROUPS)
    y = y * lax.rsqrt(jnp.mean(y * y, axis=-1, keepdims=True) + EPS)
    y = y.reshape(Bsz, S, D_INNER) * g_out.astype(jnp.float32)
    return y.astype(h.dtype) @ w_out


def mla_shared_kv(h, g_kv_in, w_dkv, g_ckv, w_ukv, cos, sin):
    Bsz, S, _ = h.shape
    hn = rms_norm(h, g_kv_in)
    ckv, k_rope = jnp.split(hn @ w_dkv, [KV_LORA], axis=-1)
    ckv = rms_norm(ckv, g_ckv)
    kv = (ckv @ w_ukv).reshape(Bsz, S, MLA_HEADS, QK_NOPE + V_HEAD)
    k_nope, v = jnp.split(kv, [QK_NOPE], axis=-1)
    k_rope = apply_rope(k_rope, cos, sin)
    return k_nope, k_rope, v


def causal_block_attention(q_nope, q_rope, k_nope, k_rope, v):
    S = q_nope.shape[1]
    scale = (QK_NOPE + QK_ROPE) ** -0.5
    local = jnp.arange(Q_BLOCK)
    outs = []
    for i in range(S // Q_BLOCK):
        q0 = i * Q_BLOCK
        kv_len = q0 + Q_BLOCK
        s = jnp.einsum('bqhd,bkhd->bhqk', q_nope[:, q0:kv_len], k_nope[:, :kv_len]) + \
            jnp.einsum('bqhr,bkr->bhqk', q_rope[:, q0:kv_len], k_rope[:, :kv_len])
        s = s.astype(jnp.float32) * scale
        mask = (q0 + local)[:, None] >= jnp.arange(kv_len)[None, :]
        p = jax.nn.softmax(jnp.where(mask, s, -jnp.inf), axis=-1).astype(v.dtype)
        outs.append(jnp.einsum('bhqk,bkhd->bqhd', p, v[:, :kv_len]))
    return jnp.concatenate(outs, axis=1)


def mla_mixer(h, k_nope, k_rope, v, cos, sin, w_in, g_q, w_uq, w_out):
    Bsz, S, _ = h.shape
    cq, gate = jnp.split(h @ w_in, [Q_LORA], axis=-1)
    q = (rms_norm(cq, g_q) @ w_uq).reshape(Bsz, S, MLA_HEADS, QK_NOPE + QK_ROPE)
    q_nope, q_rope = jnp.split(q, [QK_NOPE], axis=-1)
    q_rope = apply_rope(q_rope, cos[:, :, None, :], sin[:, :, None, :])
    o = causal_block_attention(q_nope, q_rope, k_nope, k_rope, v)
    o = o.reshape(Bsz, S, MLA_HEADS * V_HEAD) * jax.nn.silu(gate)
    return o @ w_out


def setup_inputs(seed: int = 0) -> dict:
    key = jax.random.key(seed)
    ks = jax.random.split(key, 24)
    f32 = jnp.float32

    def nrm(k, shape, fan_in):
        return jax.random.normal(k, shape, f32) * (fan_in ** -0.5)

    def gain(k, shape):
        return 1.0 + 0.01 * jax.random.normal(k, shape, f32)

    x = jax.random.normal(ks[0], (BATCH, SEQ, D_MODEL), f32)
    offset = jax.random.randint(ks[1], (BATCH, 1), 0, 4096, dtype=jnp.int32)
    positions = offset + jnp.arange(SEQ, dtype=jnp.int32)[None, :]

    g_pre = gain(ks[2], (DEPTH, D_MODEL))
    ssm_w_in = nrm(ks[3], (N_A_LAYERS, D_MODEL, SSM_IN_DIM), D_MODEL)
    ssm_conv_w = nrm(ks[4], (N_A_LAYERS, CONV_WIDTH, CONV_DIM), CONV_WIDTH)
    ssm_conv_b = 0.01 * jax.random.normal(ks[5], (N_A_LAYERS, CONV_DIM), f32)
    dt0 = jnp.exp(jax.random.uniform(ks[6], (N_A_LAYERS, SSM_HEADS), f32,
                                     np.log(1e-3), np.log(1e-1)))
    ssm_dt_bias = dt0 + jnp.log(-jnp.expm1(-dt0))
    ssm_A_log = jnp.log(jax.random.uniform(ks[7], (N_A_LAYERS, SSM_HEADS), f32, 1.0, 16.0))
    ssm_D = 1.0 + 0.1 * jax.random.normal(ks[8], (N_A_LAYERS, SSM_HEADS), f32)
    ssm_g_out = gain(ks[9], (N_A_LAYERS, D_INNER))
    ssm_w_out = nrm(ks[10], (N_A_LAYERS, D_INNER, D_MODEL), D_INNER)
    kv_g_in = gain(ks[11], (D_MODEL,))
    kv_w_down = nrm(ks[12], (D_MODEL, KV_LORA + QK_ROPE), D_MODEL)
    kv_g_latent = gain(ks[13], (KV_LORA,))
    kv_w_up = nrm(ks[14], (KV_LORA, MLA_HEADS * (QK_NOPE + V_HEAD)), KV_LORA)
    mla_w_in = nrm(ks[15], (N_B_LAYERS, D_MODEL, MLA_IN_DIM), D_MODEL)
    mla_g_q = gain(ks[16], (N_B_LAYERS, Q_LORA))
    mla_w_uq = nrm(ks[17], (N_B_LAYERS, Q_LORA, MLA_HEADS * (QK_NOPE + QK_ROPE)), Q_LORA)
    mla_w_out = nrm(ks[18], (N_B_LAYERS, MLA_HEADS * V_HEAD, D_MODEL), MLA_HEADS * V_HEAD)
    g_final = gain(ks[19], (D_MODEL,))
    return {
        'x': x, 'positions': positions, 'g_pre': g_pre,
        'ssm_w_in': ssm_w_in, 'ssm_conv_w': ssm_conv_w, 'ssm_conv_b': ssm_conv_b,
        'ssm_dt_bias': ssm_dt_bias, 'ssm_A_log': ssm_A_log, 'ssm_D': ssm_D,
        'ssm_g_out': ssm_g_out, 'ssm_w_out': ssm_w_out,
        'kv_g_in': kv_g_in, 'kv_w_down': kv_w_down, 'kv_g_latent': kv_g_latent, 'kv_w_up': kv_w_up,
        'mla_w_in': mla_w_in, 'mla_g_q': mla_g_q, 'mla_w_uq': mla_w_uq, 'mla_w_out': mla_w_out,
        'g_final': g_final,
    }


def reference(x, positions, g_pre, ssm_w_in, ssm_conv_w, ssm_conv_b, ssm_dt_bias, ssm_A_log,
              ssm_D, ssm_g_out, ssm_w_out, kv_g_in, kv_w_down, kv_g_latent, kv_w_up,
              mla_w_in, mla_g_q, mla_w_uq, mla_w_out, g_final):
    cos, sin = rope_tables(positions)
    h = x
    shared_kv = None
    for l in range(DEPTH):
        hn = rms_norm(h, g_pre[l])
        if l < N_A_LAYERS:
            h = h + mamba2_mixer(hn, ssm_w_in[l], ssm_conv_w[l], ssm_conv_b[l], ssm_dt_bias[l],
                                 ssm_A_log[l], ssm_D[l], ssm_g_out[l], ssm_w_out[l])
        else:
            if shared_kv is None:
                shared_kv = mla_shared_kv(h, kv_g_in, kv_w_down, kv_g_latent, kv_w_up, cos, sin)
            k_nope, k_rope, v = shared_kv
            j = l - N_A_LAYERS
            h = h + mla_mixer(hn, k_nope, k_rope, v, cos, sin,
                              mla_w_in[j], mla_g_q[j], mla_w_uq[j], mla_w_out[j])
    return rms_norm(h, g_final)
```

```python
import functools

import jax
import jax.numpy as jnp
from jax import lax
from jax.experimental import pallas as pl
from jax.experimental.pallas import tpu as pltpu

F32 = jnp.float32
BF16 = jnp.bfloat16

SSM_HEAD_DIM = 64
SSM_GROUPS = 4
SSM_STATE = 128
MLA_HEADS = 16
QK_NOPE = 64
QK_ROPE = 32
V_HEAD = 64
ROPE_BASE = 10000.0
EPS = 1e-6

LANES = 128
HEAD_LANES = 128
SSD_CHUNK = 256
CONV_HALO = 8
VMEM_LIMIT = 56 * 1024 * 1024


def _params(*sem):
    return pltpu.CompilerParams(dimension_semantics=sem, vmem_limit_bytes=VMEM_LIMIT)


def _silu(x):
    return x * (1.0 / (1.0 + jnp.exp(-x)))


def _softplus(x):
    return jnp.maximum(x, 0.0) + jnp.log1p(jnp.exp(-jnp.abs(x)))


def _rms_scale(x):
    return x * lax.rsqrt(jnp.mean(x * x, axis=-1, keepdims=True) + EPS)


def _dot(a, b):
    return jnp.dot(a, b, preferred_element_type=F32)


def _split2(x):
    hi = x.astype(BF16)
    lo = (x - hi.astype(F32)).astype(BF16)
    return jnp.concatenate([hi, lo], axis=-1)


def _ssm_in_proj_kernel(x_ref, g_ref, wz_ref, wx_ref, wdt_ref, z_ref, xbc_ref, dt_ref):
    hn = (_rms_scale(x_ref[...]) * g_ref[...]).astype(BF16)
    z_ref[...] = _dot(hn, wz_ref[...]).astype(BF16)
    xbc_ref[...] = _dot(hn, wx_ref[...]).astype(BF16)
    dt_ref[...] = _dot(hn, wdt_ref[...])


def _ssm_in_proj(x2d, g, wz, wx, wdt, tm):
    T, D = x2d.shape
    full = lambda a: pl.BlockSpec(a.shape, lambda i: (0, 0))
    rows = lambda n: pl.BlockSpec((tm, n), lambda i: (i, 0))
    return pl.pallas_call(
        _ssm_in_proj_kernel,
        grid=(T // tm,),
        in_specs=[rows(D), full(g), full(wz), full(wx), full(wdt)],
        out_specs=[rows(wz.shape[1]), rows(wx.shape[1]), rows(wdt.shape[1])],
        out_shape=[jax.ShapeDtypeStruct((T, wz.shape[1]), BF16),
                   jax.ShapeDtypeStruct((T, wx.shape[1]), BF16),
                   jax.ShapeDtypeStruct((T, wdt.shape[1]), F32)],
        compiler_params=_params("arbitrary"),
        name="ssm_in_proj",
    )(x2d, g, wz, wx, wdt)


def _ssd_kernel(z_ref, xbc_ref, dt_ref, x_ref, convw_ref, convb_ref, dtb_ref, alog_ref, dskip_ref,
                gout_ref, wout_ref, e2_ref, o_ref, ubuf, state, ybuf, *, d_inner, n_groups, d_state):
    Q = SSD_CHUNK
    gw = d_inner // n_groups
    heads_per_group = gw // SSM_HEAD_DIM
    pairs_per_group = gw // LANES
    kw = convw_ref.shape[0]

    @pl.when(pl.program_id(1) == 0)
    def _():
        ubuf[0:CONV_HALO, :] = jnp.zeros((CONV_HALO, ubuf.shape[1]), F32)
        state[...] = jnp.zeros(state.shape, F32)

    ubuf[CONV_HALO:CONV_HALO + Q, :] = xbc_ref[...].astype(F32)
    conv = convb_ref[...]
    for k in range(kw):
        off = CONV_HALO - (kw - 1 - k)
        conv = conv + convw_ref[k:k + 1, :] * ubuf[off:off + Q, :]
    ubuf[0:CONV_HALO, :] = ubuf[Q:Q + CONV_HALO, :]
    xbc = _silu(conv)
    xs = xbc[:, :d_inner]

    dt = _softplus(dt_ref[...] + dtb_ref[...])
    dA = dt * (-jnp.exp(alog_ref[...]))
    row = lax.broadcasted_iota(jnp.int32, (Q, Q), 0)
    col = lax.broadcasted_iota(jnp.int32, (Q, Q), 1)
    causal = row >= col
    tril = causal.astype(BF16)
    hi = dA.astype(BF16)
    mid = (dA - hi.astype(F32)).astype(BF16)
    lo = (dA - hi.astype(F32) - mid.astype(F32)).astype(BF16)
    cum = _dot(tril, hi) + _dot(tril, mid) + _dot(tril, lo)
    cum_t = cum.T
    cum_end = cum[Q - 1:Q, :]

    e2 = e2_ref[...]
    dt_x = _dot(_split2(dt), e2)
    ecum_x = _dot(_split2(jnp.exp(cum)), e2)
    wend_x = _dot(_split2(jnp.exp(cum_end - cum) * dt), e2)
    edec_x = _dot(_split2(jnp.broadcast_to(jnp.exp(cum_end), (8, LANES))), e2)[0:1, :]

    lane = lax.broadcasted_iota(jnp.int32, (Q, LANES), 1)
    first_half = lane < SSM_HEAD_DIM
    xdt = xs * dt_x

    for g in range(n_groups):
        b_g = xbc[:, d_inner + g * d_state:d_inner + (g + 1) * d_state].astype(BF16)
        c_off = d_inner + n_groups * d_state
        c_g = xbc[:, c_off + g * d_state:c_off + (g + 1) * d_state].astype(BF16)
        cb = lax.dot_general(c_g, b_g, (((1,), (1,)), ((), ())), preferred_element_type=F32)
        for p in range(pairs_per_group):
            c0 = g * gw + p * LANES
            x_pair = xdt[:, c0:c0 + LANES]
            acc = None
            for hh in range(2):
                h = g * heads_per_group + 2 * p + hh
                seg = cum[:, h:h + 1] - cum_t[h:h + 1, :]
                decay = jnp.exp(jnp.where(causal, seg, -jnp.inf))
                m = (cb * decay).astype(BF16)
                keep = first_half if hh == 0 else jnp.logical_not(first_half)
                x_h = jnp.where(keep, x_pair, 0.0).astype(BF16)
                part = _dot(m, x_h)
                acc = part if acc is None else acc + part
            ybuf[:, c0:c0 + LANES] = acc
        gs = slice(g * gw, (g + 1) * gw)
        st = state[g]
        y_inter = _dot(c_g, st.astype(BF16)) * ecum_x[:, gs]
        ybuf[:, gs] = ybuf[:, gs] + y_inter
        xw = (xs[:, gs] * wend_x[:, gs]).astype(BF16)
        upd = lax.dot_general(b_g, xw, (((0,), (0,)), ((), ())), preferred_element_type=F32)
        state[g] = st * edec_x[:, gs] + upd

    y = (ybuf[...] + dskip_ref[...] * xs) * _silu(z_ref[...].astype(F32))
    for g in range(n_groups):
        gs = slice(g * gw, (g + 1) * gw)
        ybuf[:, gs] = _rms_scale(y[:, gs])
    yn = (ybuf[...] * gout_ref[...]).astype(BF16)
    o_ref[...] = x_ref[...] + _dot(yn, wout_ref[...])


def _ssd_mixer(z, xbc, dt_raw, x2d, convw, convb, dtb, alog, dskip, gout, wout, e2, batch):
    T, d_inner = z.shape
    D = x2d.shape[1]
    conv_dim = xbc.shape[1]
    n_chunks = T // batch // SSD_CHUNK
    gw = d_inner // SSM_GROUPS
    full = lambda a: pl.BlockSpec(a.shape, lambda b, c: (0, 0))
    rows = lambda n: pl.BlockSpec((SSD_CHUNK, n), lambda b, c: (b * n_chunks + c, 0))
    kernel = functools.partial(_ssd_kernel, d_inner=d_inner, n_groups=SSM_GROUPS, d_state=SSM_STATE)
    return pl.pallas_call(
        kernel,
        grid=(batch, n_chunks),
        in_specs=[rows(d_inner), rows(conv_dim), rows(LANES), rows(D), full(convw), full(convb), full(dtb),
                  full(alog), full(dskip), full(gout), full(wout), full(e2)],
        out_specs=rows(D),
        out_shape=jax.ShapeDtypeStruct((T, D), F32),
        scratch_shapes=[pltpu.VMEM((SSD_CHUNK + CONV_HALO, conv_dim), F32),
                        pltpu.VMEM((SSM_GROUPS, SSM_STATE, gw), F32),
                        pltpu.VMEM((SSD_CHUNK, d_inner), F32)],
        compiler_params=_params("arbitrary", "arbitrary"),
        name="ssd_mixer",
    )(z, xbc, dt_raw, x2d, convw, convb, dtb, alog, dskip, gout, wout, e2)


def _mla_proj_kernel(h_ref, pos_ref, invf_ref, gkv_ref, gq_ref, wdc_ref, wdr_ref, gckv_ref, wuk_ref, wuv_ref,
                     wcq_ref, wgate_ref, gqn_ref, wuq_ref, wuqs_ref, q_ref, k_ref, v_ref, gate_ref, *, scale):
    hs = _rms_scale(h_ref[...])
    hkv = (hs * gkv_ref[...]).astype(BF16)
    hq = (hs * gq_ref[...]).astype(BF16)

    ang = pos_ref[...].astype(F32) * invf_ref[...]
    cos = jnp.cos(ang)
    sin = jnp.sin(ang)

    ckv = (_rms_scale(_dot(hkv, wdc_ref[...])) * gckv_ref[...]).astype(BF16)
    kr2 = _dot(hkv, wdr_ref[...])
    k_rope = kr2[:, :HEAD_LANES] * cos + kr2[:, HEAD_LANES:] * sin
    v_ref[...] = _dot(ckv, wuv_ref[...]).astype(BF16)
    k_nope = _dot(ckv, wuk_ref[...])

    gate_ref[...] = _dot(hq, wgate_ref[...]).astype(BF16)
    cq = (_rms_scale(_dot(hq, wcq_ref[...])) * gqn_ref[...]).astype(BF16)
    qm = _dot(cq, wuq_ref[...])
    qs = _dot(cq, wuqs_ref[...])
    cos_q = cos * scale
    sin_q = sin * scale
    for h in range(q_ref.shape[1] // HEAD_LANES):
        sl = slice(h * HEAD_LANES, (h + 1) * HEAD_LANES)
        k_ref[:, sl] = (k_nope[:, sl] + k_rope).astype(BF16)
        q_ref[:, sl] = (qm[:, sl] * cos_q + qs[:, sl] * sin_q).astype(BF16)


def _mla_proj(h1, pos, invf, gkv, gq, wdc, wdr, gckv, wuk, wuv, wcq, wgate, gqn, wuq, wuqs, tm, scale):
    T, D = h1.shape
    full = lambda a: pl.BlockSpec(a.shape, lambda i: (0, 0))
    rows = lambda n: pl.BlockSpec((tm, n), lambda i: (i, 0))
    hq = wuq.shape[1]
    hv = wuv.shape[1]
    hg = wgate.shape[1]
    weights = (invf, gkv, gq, wdc, wdr, gckv, wuk, wuv, wcq, wgate, gqn, wuq, wuqs)
    return pl.pallas_call(
        functools.partial(_mla_proj_kernel, scale=scale),
        grid=(T // tm,),
        in_specs=[rows(D), rows(1)] + [full(w) for w in weights],
        out_specs=[rows(hq), rows(hq), rows(hv), rows(hg)],
        out_shape=[jax.ShapeDtypeStruct((T, hq), BF16), jax.ShapeDtypeStruct((T, hq), BF16),
                   jax.ShapeDtypeStruct((T, hv), BF16), jax.ShapeDtypeStruct((T, hg), BF16)],
        compiler_params=_params("arbitrary"),
        name="mla_proj",
    )(h1, pos, *weights)


def _attn_kernel(q_ref, k_ref, v_ref, gate_ref, o_ref, *, tq):
    qi = pl.program_id(2)
    row = lax.broadcasted_iota(jnp.int32, (tq, tq), 0)
    col = lax.broadcasted_iota(jnp.int32, (tq, tq), 1)
    causal = row >= col
    lane = lax.broadcasted_iota(jnp.int32, (tq, LANES), 1)

    outs = []
    for hh in range(2):
        hs = slice(hh * HEAD_LANES, (hh + 1) * HEAD_LANES)
        q = q_ref[:, hs]

        def step(j, carry, masked, hs=hs, q=q):
            m, l, acc = carry
            ks = pl.ds(pl.multiple_of(j * tq, tq), tq)
            s = lax.dot_general(q, k_ref[ks, hs], (((1,), (1,)), ((), ())), preferred_element_type=F32)
            if masked:
                s = jnp.where(causal, s, -jnp.inf)
            m_new = jnp.maximum(m, jnp.max(s, axis=-1, keepdims=True))
            alpha = jnp.exp(m - m_new)
            p = jnp.exp(s - m_new)
            l = alpha * l + jnp.sum(p, axis=-1, keepdims=True)
            acc = alpha * acc + _dot(p.astype(BF16), v_ref[ks, :])
            return m_new, l, acc

        init = (jnp.full((tq, 1), -jnp.inf, F32), jnp.zeros((tq, 1), F32), jnp.zeros((tq, LANES), F32))
        carry = lax.fori_loop(0, qi, functools.partial(step, masked=False), init)
        m, l, acc = step(qi, carry, masked=True)
        outs.append(acc * (1.0 / l))
    o = jnp.where(lane < V_HEAD, outs[0], outs[1])
    o_ref[...] = (o * _silu(gate_ref[...].astype(F32))).astype(BF16)


def _mla_attn(q, k, v, gate, batch, tq):
    T = q.shape[0]
    S = T // batch
    n_pairs = v.shape[1] // LANES
    n_q = S // tq
    return pl.pallas_call(
        functools.partial(_attn_kernel, tq=tq),
        grid=(batch, n_pairs, n_q),
        in_specs=[pl.BlockSpec((tq, 2 * HEAD_LANES), lambda b, p, i: (b * n_q + i, p)),
                  pl.BlockSpec((S, 2 * HEAD_LANES), lambda b, p, i: (b, p)),
                  pl.BlockSpec((S, LANES), lambda b, p, i: (b, p)),
                  pl.BlockSpec((tq, LANES), lambda b, p, i: (b * n_q + i, p))],
        out_specs=pl.BlockSpec((tq, LANES), lambda b, p, i: (b * n_q + i, p)),
        out_shape=jax.ShapeDtypeStruct((T, v.shape[1]), BF16),
        compiler_params=_params("arbitrary", "arbitrary", "arbitrary"),
        name="mla_attn",
    )(q, k, v, gate)


def _mla_out_kernel(o_ref, h_ref, w_ref, g_ref, out_ref):
    h = h_ref[...] + _dot(o_ref[...], w_ref[...])
    out_ref[...] = _rms_scale(h) * g_ref[...]


def _mla_out(o, h1, w, g, tm):
    T, D = h1.shape
    full = lambda a: pl.BlockSpec(a.shape, lambda i: (0, 0))
    rows = lambda n: pl.BlockSpec((tm, n), lambda i: (i, 0))
    return pl.pallas_call(
        _mla_out_kernel,
        grid=(T // tm,),
        in_specs=[rows(o.shape[1]), rows(D), full(w), full(g)],
        out_specs=rows(D),
        out_shape=jax.ShapeDtypeStruct((T, D), F32),
        compiler_params=_params("arbitrary"),
        name="mla_out",
    )(o, h1, w, g)


def _pad_cols(a, n):
    return jnp.pad(a, ((0, 0), (0, n - a.shape[1])))


def _head_slots(w_nope, w_rope):
    k, h = w_nope.shape[0], w_nope.shape[1]
    pad = jnp.zeros((k, h, HEAD_LANES - QK_NOPE - QK_ROPE), w_nope.dtype)
    return jnp.concatenate([w_nope, w_rope, pad], axis=-1).reshape(k, h * HEAD_LANES)


def _rotate_half_cols(w_rope):
    w1, w2 = jnp.split(w_rope, 2, axis=-1)
    return jnp.concatenate([-w2, w1], axis=-1)


def kernel(x, positions, g_pre, ssm_w_in, ssm_conv_w, ssm_conv_b, ssm_dt_bias, ssm_A_log, ssm_D, ssm_g_out,
           ssm_w_out, kv_g_in, kv_w_down, kv_g_latent, kv_w_up, mla_w_in, mla_g_q, mla_w_uq, mla_w_out, g_final):
    B, S, D = x.shape
    T = B * S
    d_inner = ssm_w_out.shape[1]
    n_heads = ssm_dt_bias.shape[1]
    conv_dim = ssm_conv_w.shape[2]
    kv_lora = kv_g_latent.shape[0]
    q_lora = mla_g_q.shape[1]
    H = MLA_HEADS
    assert d_inner == n_heads * SSM_HEAD_DIM and n_heads <= LANES
    assert S % SSD_CHUNK == 0 and ssm_conv_w.shape[1] - 1 <= CONV_HALO
    x2d = x.reshape(T, D)

    w_in = ssm_w_in[0]
    wz = w_in[:, :d_inner].astype(BF16)
    wx = w_in[:, d_inner:d_inner + conv_dim].astype(BF16)
    wdt = _pad_cols(w_in[:, d_inner + conv_dim:], LANES).astype(BF16)
    z, xbc, dt_raw = _ssm_in_proj(x2d, g_pre[0][None, :], wz, wx, wdt, tm=512)

    head_of_channel = jnp.arange(d_inner) // SSM_HEAD_DIM
    expand = (jnp.arange(LANES)[:, None] == head_of_channel[None, :]).astype(BF16)
    e2 = jnp.concatenate([expand, expand], axis=0)
    h1 = _ssd_mixer(
        z, xbc, dt_raw, x2d, ssm_conv_w[0], ssm_conv_b[0][None, :],
        _pad_cols(ssm_dt_bias[0][None, :], LANES), _pad_cols(ssm_A_log[0][None, :], LANES),
        jnp.repeat(ssm_D[0], SSM_HEAD_DIM)[None, :], ssm_g_out[0][None, :], ssm_w_out[0].astype(BF16), e2, batch=B)

    inv = ROPE_BASE ** (-jnp.arange(0, QK_ROPE, 2, dtype=F32) / QK_ROPE)
    invf = jnp.concatenate([jnp.zeros((QK_NOPE,), F32), inv, inv,
                            jnp.zeros((HEAD_LANES - QK_NOPE - QK_ROPE,), F32)])[None, :]
    wdc = kv_w_down[:, :kv_lora].astype(BF16)
    wr = kv_w_down[:, kv_lora:]
    slot1 = lambda w: jnp.pad(w, ((0, 0), (QK_NOPE, HEAD_LANES - QK_NOPE - QK_ROPE)))
    wdr = jnp.concatenate([slot1(wr), slot1(_rotate_half_cols(wr))], axis=1).astype(BF16)
    wup = kv_w_up.reshape(kv_lora, H, QK_NOPE + V_HEAD)
    wuk = _head_slots(wup[:, :, :QK_NOPE], jnp.zeros((kv_lora, H, QK_ROPE), F32)).astype(BF16)
    wuv = wup[:, :, QK_NOPE:].reshape(kv_lora, H * V_HEAD).astype(BF16)
    w_in_b = mla_w_in[0]
    wcq = w_in_b[:, :q_lora].astype(BF16)
    wgate = w_in_b[:, q_lora:].astype(BF16)
    wq = mla_w_uq[0].reshape(q_lora, H, QK_NOPE + QK_ROPE)
    wq_nope, wq_rope = wq[:, :, :QK_NOPE], wq[:, :, QK_NOPE:]
    wuq = _head_slots(wq_nope, wq_rope).astype(BF16)
    wuqs = _head_slots(jnp.zeros_like(wq_nope), _rotate_half_cols(wq_rope)).astype(BF16)
    scale = float((QK_NOPE + QK_ROPE) ** -0.5)

    q, k, v, gate = _mla_proj(
        h1, positions.reshape(T, 1), invf, kv_g_in[None, :], g_pre[1][None, :], wdc, wdr, kv_g_latent[None, :],
        wuk, wuv, wcq, wgate, mla_g_q[0][None, :], wuq, wuqs, tm=512, scale=scale)
    o = _mla_attn(q, k, v, gate, batch=B, tq=256)
    out = _mla_out(o, h1, mla_w_out[0].astype(BF16), g_final[None, :], tm=512)
    return out.reshape(B, S, D)
```

```python
import functools
import math

import jax
import jax.numpy as jnp
from jax import lax
from jax.experimental import pallas as pl
from jax.experimental.pallas import tpu as pltpu

F32 = jnp.float32
BF16 = jnp.bfloat16

SSM_HEAD_DIM = 64
SSM_GROUPS = 4
SSM_STATE = 128
MLA_HEADS = 16
QK_NOPE = 64
QK_ROPE = 32
V_HEAD = 64
ROPE_BASE = 10000.0
EPS = 1e-6

LANES = 128
HEAD_LANES = 128
SSD_CHUNK = 256
CONV_HALO = 8
VMEM_LIMIT = 56 * 1024 * 1024


def _params(*sem):
    return pltpu.CompilerParams(dimension_semantics=sem, vmem_limit_bytes=VMEM_LIMIT)


def _silu(x):
    return x * (1.0 / (1.0 + jnp.exp(-x)))


def _softplus(x):
    return jnp.maximum(x, 0.0) + jnp.log1p(jnp.exp(-jnp.abs(x)))


def _rms_scale(x):
    return x * lax.rsqrt(jnp.mean(x * x, axis=-1, keepdims=True) + EPS)


def _dot(a, b):
    return jnp.dot(a, b, preferred_element_type=F32)


def _split2(x):
    hi = x.astype(BF16)
    lo = (x - hi.astype(F32)).astype(BF16)
    return jnp.concatenate([hi, lo], axis=-1)


def _ssm_in_proj_kernel(x_ref, g_ref, wz_ref, wx_ref, wdt_ref, z_ref, xbc_ref, dt_ref):
    hn = (_rms_scale(x_ref[...]) * g_ref[...]).astype(BF16)
    z_ref[...] = _dot(hn, wz_ref[...]).astype(BF16)
    xbc_ref[...] = _dot(hn, wx_ref[...]).astype(BF16)
    dt_ref[...] = _dot(hn, wdt_ref[...])


def _ssm_in_proj(x2d, g, wz, wx, wdt, tm):
    T, D = x2d.shape
    full = lambda a: pl.BlockSpec(a.shape, lambda i: (0, 0))
    rows = lambda n: pl.BlockSpec((tm, n), lambda i: (i, 0))
    return pl.pallas_call(
        _ssm_in_proj_kernel,
        grid=(T // tm,),
        in_specs=[rows(D), full(g), full(wz), full(wx), full(wdt)],
        out_specs=[rows(wz.shape[1]), rows(wx.shape[1]), rows(wdt.shape[1])],
        out_shape=[jax.ShapeDtypeStruct((T, wz.shape[1]), BF16),
                   jax.ShapeDtypeStruct((T, wx.shape[1]), BF16),
                   jax.ShapeDtypeStruct((T, wdt.shape[1]), F32)],
        compiler_params=_params("arbitrary"),
        name="ssm_in_proj",
    )(x2d, g, wz, wx, wdt)


def _ssd_kernel(z_ref, xbc_ref, dt_ref, x_ref, convw_ref, convb_ref, dtb_ref, alog_ref, dskip_ref,
                gout_ref, wout_ref, e2_ref, o_ref, ubuf, state, ybuf, *, d_inner, n_groups, d_state):
    Q = SSD_CHUNK
    gw = d_inner // n_groups
    heads_per_group = gw // SSM_HEAD_DIM
    pairs_per_group = gw // LANES
    kw = convw_ref.shape[0]

    @pl.when(pl.program_id(1) == 0)
    def _():
        ubuf[0:CONV_HALO, :] = jnp.zeros((CONV_HALO, ubuf.shape[1]), F32)
        state[...] = jnp.zeros(state.shape, F32)

    ubuf[CONV_HALO:CONV_HALO + Q, :] = xbc_ref[...].astype(F32)
    conv = convb_ref[...]
    for k in range(kw):
        off = CONV_HALO - (kw - 1 - k)
        conv = conv + convw_ref[k:k + 1, :] * ubuf[off:off + Q, :]
    ubuf[0:CONV_HALO, :] = ubuf[Q:Q + CONV_HALO, :]
    xbc = _silu(conv)
    xs = xbc[:, :d_inner]

    dt = _softplus(dt_ref[...] + dtb_ref[...])
    dA = dt * (-jnp.exp(alog_ref[...]))
    row = lax.broadcasted_iota(jnp.int32, (Q, Q), 0)
    col = lax.broadcasted_iota(jnp.int32, (Q, Q), 1)
    causal = row >= col
    tril = causal.astype(BF16)
    hi = dA.astype(BF16)
    mid = (dA - hi.astype(F32)).astype(BF16)
    lo = (dA - hi.astype(F32) - mid.astype(F32)).astype(BF16)
    cum = _dot(tril, hi) + _dot(tril, mid) + _dot(tril, lo)
    cum_t = cum.T
    cum_end = cum[Q - 1:Q, :]

    e2 = e2_ref[...]
    dt_x = _dot(_split2(dt), e2)
    ecum_x = _dot(_split2(jnp.exp(cum)), e2)
    wend_x = _dot(_split2(jnp.exp(cum_end - cum) * dt), e2)
    edec_x = _dot(_split2(jnp.broadcast_to(jnp.exp(cum_end), (8, LANES))), e2)[0:1, :]

    lane = lax.broadcasted_iota(jnp.int32, (Q, LANES), 1)
    first_half = lane < SSM_HEAD_DIM
    xdt = xs * dt_x

    for g in range(n_groups):
        b_g = xbc[:, d_inner + g * d_state:d_inner + (g + 1) * d_state].astype(BF16)
        c_off = d_inner + n_groups * d_state
        c_g = xbc[:, c_off + g * d_state:c_off + (g + 1) * d_state].astype(BF16)
        cb = lax.dot_general(c_g, b_g, (((1,), (1,)), ((), ())), preferred_element_type=F32)
        for p in range(pairs_per_group):
            c0 = g * gw + p * LANES
            x_pair = xdt[:, c0:c0 + LANES]
            acc = None
            for hh in range(2):
                h = g * heads_per_group + 2 * p + hh
                seg = cum[:, h:h + 1] - cum_t[h:h + 1, :]
                decay = jnp.exp(jnp.where(causal, seg, -jnp.inf))
                m = (cb * decay).astype(BF16)
                keep = first_half if hh == 0 else jnp.logical_not(first_half)
                x_h = jnp.where(keep, x_pair, 0.0).astype(BF16)
                part = _dot(m, x_h)
                acc = part if acc is None else acc + part
            ybuf[:, c0:c0 + LANES] = acc
        gs = slice(g * gw, (g + 1) * gw)
        st = state[g]
        y_inter = _dot(c_g, st.astype(BF16)) * ecum_x[:, gs]
        ybuf[:, gs] = ybuf[:, gs] + y_inter
        xw = (xs[:, gs] * wend_x[:, gs]).astype(BF16)
        upd = lax.dot_general(b_g, xw, (((0,), (0,)), ((), ())), preferred_element_type=F32)
        state[g] = st * edec_x[:, gs] + upd

    y = (ybuf[...] + dskip_ref[...] * xs) * _silu(z_ref[...].astype(F32))
    for g in range(n_groups):
        gs = slice(g * gw, (g + 1) * gw)
        ybuf[:, gs] = _rms_scale(y[:, gs])
    yn = (ybuf[...] * gout_ref[...]).astype(BF16)
    o_ref[...] = x_ref[...] + _dot(yn, wout_ref[...])


def _ssd_mixer(z, xbc, dt_raw, x2d, convw, convb, dtb, alog, dskip, gout, wout, e2, batch):
    T, d_inner = z.shape
    D = x2d.shape[1]
    conv_dim = xbc.shape[1]
    n_chunks = T // batch // SSD_CHUNK
    gw = d_inner // SSM_GROUPS
    full = lambda a: pl.BlockSpec(a.shape, lambda b, c: (0, 0))
    rows = lambda n: pl.BlockSpec((SSD_CHUNK, n), lambda b, c: (b * n_chunks + c, 0))
    kernel = functools.partial(_ssd_kernel, d_inner=d_inner, n_groups=SSM_GROUPS, d_state=SSM_STATE)
    return pl.pallas_call(
        kernel,
        grid=(batch, n_chunks),
        in_specs=[rows(d_inner), rows(conv_dim), rows(LANES), rows(D), full(convw), full(convb), full(dtb),
                  full(alog), full(dskip), full(gout), full(wout), full(e2)],
        out_specs=rows(D),
        out_shape=jax.ShapeDtypeStruct((T, D), F32),
        scratch_shapes=[pltpu.VMEM((SSD_CHUNK + CONV_HALO, conv_dim), F32),
                        pltpu.VMEM((SSM_GROUPS, SSM_STATE, gw), F32),
                        pltpu.VMEM((SSD_CHUNK, d_inner), F32)],
        compiler_params=_params("arbitrary", "arbitrary"),
        name="ssd_mixer",
    )(z, xbc, dt_raw, x2d, convw, convb, dtb, alog, dskip, gout, wout, e2)


def _mla_proj_kernel(h_ref, pos_ref, invf_ref, gkv_ref, gq_ref, wdc_ref, wdr_ref, gckv_ref, wuk_ref, wuv_ref,
                     wcq_ref, wgate_ref, gqn_ref, wuq_ref, wuqs_ref, q_ref, k_ref, v_ref, gate_ref, *, scale):
    hs = _rms_scale(h_ref[...])
    hkv = (hs * gkv_ref[...]).astype(BF16)
    hq = (hs * gq_ref[...]).astype(BF16)

    ang = pos_ref[...].astype(F32) * invf_ref[...]
    cos = jnp.cos(ang)
    sin = jnp.sin(ang)

    ckv = (_rms_scale(_dot(hkv, wdc_ref[...])) * gckv_ref[...]).astype(BF16)
    kr2 = _dot(hkv, wdr_ref[...])
    k_rope = kr2[:, :HEAD_LANES] * cos + kr2[:, HEAD_LANES:] * sin
    v_ref[...] = _dot(ckv, wuv_ref[...]).astype(BF16)
    k_nope = _dot(ckv, wuk_ref[...])

    gate_ref[...] = _dot(hq, wgate_ref[...]).astype(BF16)
    cq = (_rms_scale(_dot(hq, wcq_ref[...])) * gqn_ref[...]).astype(BF16)
    qm = _dot(cq, wuq_ref[...])
    qs = _dot(cq, wuqs_ref[...])
    cos_q = cos * scale
    sin_q = sin * scale
    for h in range(q_ref.shape[1] // HEAD_LANES):
        sl = slice(h * HEAD_LANES, (h + 1) * HEAD_LANES)
        k_ref[:, sl] = (k_nope[:, sl] + k_rope).astype(BF16)
        q_ref[:, sl] = (qm[:, sl] * cos_q + qs[:, sl] * sin_q).astype(BF16)


def _mla_proj(h1, pos, invf, gkv, gq, wdc, wdr, gckv, wuk, wuv, wcq, wgate, gqn, wuq, wuqs, tm, scale):
    T, D = h1.shape
    full = lambda a: pl.BlockSpec(a.shape, lambda i: (0, 0))
    rows = lambda n: pl.BlockSpec((tm, n), lambda i: (i, 0))
    hq = wuq.shape[1]
    hv = wuv.shape[1]
    hg = wgate.shape[1]
    weights = (invf, gkv, gq, wdc, wdr, gckv, wuk, wuv, wcq, wgate, gqn, wuq, wuqs)
    return pl.pallas_call(
        functools.partial(_mla_proj_kernel, scale=scale),
        grid=(T // tm,),
        in_specs=[rows(D), rows(1)] + [full(w) for w in weights],
        out_specs=[rows(hq), rows(hq), rows(hv), rows(hg)],
        out_shape=[jax.ShapeDtypeStruct((T, hq), BF16), jax.ShapeDtypeStruct((T, hq), BF16),
                   jax.ShapeDtypeStruct((T, hv), BF16), jax.ShapeDtypeStruct((T, hg), BF16)],
        compiler_params=_params("arbitrary"),
        name="mla_proj",
    )(h1, pos, *weights)


def _attn_kernel(q_ref, k_ref, v_ref, gate_ref, o_ref, *, tq):
    qi = pl.program_id(2)
    row = lax.broadcasted_iota(jnp.int32, (tq, tq), 0)
    col = lax.broadcasted_iota(jnp.int32, (tq, tq), 1)
    lane = lax.broadcasted_iota(jnp.int32, (tq, LANES), 1)
    heads = [slice(hh * HEAD_LANES, (hh + 1) * HEAD_LANES) for hh in range(2)]
    qs = [q_ref[:, hs] for hs in heads]

    def step(j, carry, masked):
        ks = pl.ds(pl.multiple_of(j * tq, tq), tq)
        v = v_ref[ks, :]
        out = []
        for hs, q, (m, l, acc) in zip(heads, qs, carry):
            s = lax.dot_general(q, k_ref[ks, hs], (((1,), (1,)), ((), ())), preferred_element_type=F32)
            if masked:
                s = jnp.where(row >= col, s, -jnp.inf)
            m_new = jnp.maximum(m, jnp.max(s, axis=-1, keepdims=True))
            alpha = jnp.exp2(m - m_new)
            p = jnp.exp2(s - m_new)
            psum = p[:, 0:LANES]
            for c in range(1, tq // LANES):
                psum = psum + p[:, c * LANES:(c + 1) * LANES]
            l = alpha * l + psum
            acc = alpha * acc + _dot(p.astype(BF16), v)
            out.append((m_new, l, acc))
        return tuple(out)

    init = tuple((jnp.full((tq, 1), -jnp.inf, F32), jnp.zeros((tq, LANES), F32), jnp.zeros((tq, LANES), F32))
                 for _ in heads)
    carry = lax.fori_loop(0, qi, functools.partial(step, masked=False), init)
    carry = step(qi, carry, masked=True)
    outs = [acc * (1.0 / jnp.sum(l, axis=-1, keepdims=True)) for (_, l, acc) in carry]
    o = jnp.where(lane < V_HEAD, outs[0], outs[1])
    o_ref[...] = (o * _silu(gate_ref[...].astype(F32))).astype(BF16)


def _mla_attn(q, k, v, gate, batch, tq):
    T = q.shape[0]
    S = T // batch
    n_pairs = v.shape[1] // LANES
    n_q = S // tq
    return pl.pallas_call(
        functools.partial(_attn_kernel, tq=tq),
        grid=(batch, n_pairs, n_q),
        in_specs=[pl.BlockSpec((tq, 2 * HEAD_LANES), lambda b, p, i: (b * n_q + i, p)),
                  pl.BlockSpec((S, 2 * HEAD_LANES), lambda b, p, i: (b, p)),
                  pl.BlockSpec((S, LANES), lambda b, p, i: (b, p)),
                  pl.BlockSpec((tq, LANES), lambda b, p, i: (b * n_q + i, p))],
        out_specs=pl.BlockSpec((tq, LANES), lambda b, p, i: (b * n_q + i, p)),
        out_shape=jax.ShapeDtypeStruct((T, v.shape[1]), BF16),
        compiler_params=_params("arbitrary", "arbitrary", "arbitrary"),
        name="mla_attn",
    )(q, k, v, gate)


def _mla_out_kernel(o_ref, h_ref, w_ref, g_ref, out_ref):
    h = h_ref[...] + _dot(o_ref[...], w_ref[...])
    out_ref[...] = _rms_scale(h) * g_ref[...]


def _mla_out(o, h1, w, g, tm):
    T, D = h1.shape
    full = lambda a: pl.BlockSpec(a.shape, lambda i: (0, 0))
    rows = lambda n: pl.BlockSpec((tm, n), lambda i: (i, 0))
    return pl.pallas_call(
        _mla_out_kernel,
        grid=(T // tm,),
        in_specs=[rows(o.shape[1]), rows(D), full(w), full(g)],
        out_specs=rows(D),
        out_shape=jax.ShapeDtypeStruct((T, D), F32),
        compiler_params=_params("arbitrary"),
        name="mla_out",
    )(o, h1, w, g)


def _pad_cols(a, n):
    return jnp.pad(a, ((0, 0), (0, n - a.shape[1])))


def _head_slots(w_nope, w_rope):
    k, h = w_nope.shape[0], w_nope.shape[1]
    pad = jnp.zeros((k, h, HEAD_LANES - QK_NOPE - QK_ROPE), w_nope.dtype)
    return jnp.concatenate([w_nope, w_rope, pad], axis=-1).reshape(k, h * HEAD_LANES)


def _rotate_half_cols(w_rope):
    w1, w2 = jnp.split(w_rope, 2, axis=-1)
    return jnp.concatenate([-w2, w1], axis=-1)


def kernel(x, positions, g_pre, ssm_w_in, ssm_conv_w, ssm_conv_b, ssm_dt_bias, ssm_A_log, ssm_D, ssm_g_out,
           ssm_w_out, kv_g_in, kv_w_down, kv_g_latent, kv_w_up, mla_w_in, mla_g_q, mla_w_uq, mla_w_out, g_final):
    B, S, D = x.shape
    T = B * S
    d_inner = ssm_w_out.shape[1]
    n_heads = ssm_dt_bias.shape[1]
    conv_dim = ssm_conv_w.shape[2]
    kv_lora = kv_g_latent.shape[0]
    q_lora = mla_g_q.shape[1]
    H = MLA_HEADS
    assert d_inner == n_heads * SSM_HEAD_DIM and n_heads <= LANES
    assert S % SSD_CHUNK == 0 and ssm_conv_w.shape[1] - 1 <= CONV_HALO
    x2d = x.reshape(T, D)

    w_in = ssm_w_in[0]
    wz = w_in[:, :d_inner].astype(BF16)
    wx = w_in[:, d_inner:d_inner + conv_dim].astype(BF16)
    wdt = _pad_cols(w_in[:, d_inner + conv_dim:], LANES).astype(BF16)
    z, xbc, dt_raw = _ssm_in_proj(x2d, g_pre[0][None, :], wz, wx, wdt, tm=512)

    head_of_channel = jnp.arange(d_inner) // SSM_HEAD_DIM
    expand = (jnp.arange(LANES)[:, None] == head_of_channel[None, :]).astype(BF16)
    e2 = jnp.concatenate([expand, expand], axis=0)
    h1 = _ssd_mixer(
        z, xbc, dt_raw, x2d, ssm_conv_w[0], ssm_conv_b[0][None, :],
        _pad_cols(ssm_dt_bias[0][None, :], LANES), _pad_cols(ssm_A_log[0][None, :], LANES),
        jnp.repeat(ssm_D[0], SSM_HEAD_DIM)[None, :], ssm_g_out[0][None, :], ssm_w_out[0].astype(BF16), e2, batch=B)

    inv = ROPE_BASE ** (-jnp.arange(0, QK_ROPE, 2, dtype=F32) / QK_ROPE)
    invf = jnp.concatenate([jnp.zeros((QK_NOPE,), F32), inv, inv,
                            jnp.zeros((HEAD_LANES - QK_NOPE - QK_ROPE,), F32)])[None, :]
    wdc = kv_w_down[:, :kv_lora].astype(BF16)
    wr = kv_w_down[:, kv_lora:]
    slot1 = lambda w: jnp.pad(w, ((0, 0), (QK_NOPE, HEAD_LANES - QK_NOPE - QK_ROPE)))
    wdr = jnp.concatenate([slot1(wr), slot1(_rotate_half_cols(wr))], axis=1).astype(BF16)
    wup = kv_w_up.reshape(kv_lora, H, QK_NOPE + V_HEAD)
    wuk = _head_slots(wup[:, :, :QK_NOPE], jnp.zeros((kv_lora, H, QK_ROPE), F32)).astype(BF16)
    wuv = wup[:, :, QK_NOPE:].reshape(kv_lora, H * V_HEAD).astype(BF16)
    w_in_b = mla_w_in[0]
    wcq = w_in_b[:, :q_lora].astype(BF16)
    wgate = w_in_b[:, q_lora:].astype(BF16)
    wq = mla_w_uq[0].reshape(q_lora, H, QK_NOPE + QK_ROPE)
    wq_nope, wq_rope = wq[:, :, :QK_NOPE], wq[:, :, QK_NOPE:]
    wuq = _head_slots(wq_nope, wq_rope).astype(BF16)
    wuqs = _head_slots(jnp.zeros_like(wq_nope), _rotate_half_cols(wq_rope)).astype(BF16)
    scale = float((QK_NOPE + QK_ROPE) ** -0.5 * math.log2(math.e))

    q, k, v, gate = _mla_proj(
        h1, positions.reshape(T, 1), invf, kv_g_in[None, :], g_pre[1][None, :], wdc, wdr, kv_g_latent[None, :],
        wuk, wuv, wcq, wgate, mla_g_q[0][None, :], wuq, wuqs, tm=512, scale=scale)
    o = _mla_attn(q, k, v, gate, batch=B, tq=512)
    out = _mla_out(o, h1, mla_w_out[0].astype(BF16), g_final[None, :], tm=512)
    return out.reshape(B, S, D)
```

```python
import functools
import math

import jax
import jax.numpy as jnp
from jax import lax
from jax.experimental import pallas as pl
from jax.experimental.pallas import tpu as pltpu

F32 = jnp.float32
BF16 = jnp.bfloat16

SSM_HEAD_DIM = 64
SSM_GROUPS = 4
SSM_STATE = 128
MLA_HEADS = 16
QK_NOPE = 64
QK_ROPE = 32
V_HEAD = 64
ROPE_BASE = 10000.0
EPS = 1e-6

LANES = 128
HEAD_LANES = 128
SSD_CHUNK = 256
CONV_HALO = 8
VMEM_LIMIT = 56 * 1024 * 1024


def _params(*sem):
    return pltpu.CompilerParams(dimension_semantics=sem, vmem_limit_bytes=VMEM_LIMIT)


def _silu(x):
    return x * (1.0 / (1.0 + jnp.exp(-x)))


def _softplus(x):
    return jnp.maximum(x, 0.0) + jnp.log1p(jnp.exp(-jnp.abs(x)))


def _rms_scale(x):
    return x * lax.rsqrt(jnp.mean(x * x, axis=-1, keepdims=True) + EPS)


def _dot(a, b):
    return jnp.dot(a, b, preferred_element_type=F32)


def _split2(x):
    hi = x.astype(BF16)
    lo = (x - hi.astype(F32)).astype(BF16)
    return jnp.concatenate([hi, lo], axis=-1)


def _ssm_in_proj_kernel(x_ref, g_ref, wz_ref, wx_ref, wdt_ref, z_ref, xbc_ref, dt_ref):
    hn = (_rms_scale(x_ref[...]) * g_ref[...]).astype(BF16)
    z_ref[...] = _dot(hn, wz_ref[...]).astype(BF16)
    xbc_ref[...] = _dot(hn, wx_ref[...]).astype(BF16)
    dt_ref[...] = _dot(hn, wdt_ref[...])


def _ssm_in_proj(x2d, g, wz, wx, wdt, tm):
    T, D = x2d.shape
    full = lambda a: pl.BlockSpec(a.shape, lambda i: (0, 0))
    rows = lambda n: pl.BlockSpec((tm, n), lambda i: (i, 0))
    return pl.pallas_call(
        _ssm_in_proj_kernel,
        grid=(T // tm,),
        in_specs=[rows(D), full(g), full(wz), full(wx), full(wdt)],
        out_specs=[rows(wz.shape[1]), rows(wx.shape[1]), rows(wdt.shape[1])],
        out_shape=[jax.ShapeDtypeStruct((T, wz.shape[1]), BF16),
                   jax.ShapeDtypeStruct((T, wx.shape[1]), BF16),
                   jax.ShapeDtypeStruct((T, wdt.shape[1]), F32)],
        compiler_params=_params("arbitrary"),
        name="ssm_in_proj",
    )(x2d, g, wz, wx, wdt)


def _ssd_kernel(z_ref, xbc_ref, dt_ref, x_ref, convw_ref, convb_ref, dtb_ref, alog_ref, dskip_ref,
                gout_ref, wout_ref, e2_ref, o_ref, ubuf, state, ybuf, *, d_inner, n_groups, d_state):
    Q = SSD_CHUNK
    gw = d_inner // n_groups
    heads_per_group = gw // SSM_HEAD_DIM
    pairs_per_group = gw // LANES
    kw = convw_ref.shape[0]

    @pl.when(pl.program_id(1) == 0)
    def _():
        ubuf[0:CONV_HALO, :] = jnp.zeros((CONV_HALO, ubuf.shape[1]), F32)
        state[...] = jnp.zeros(state.shape, F32)

    ubuf[CONV_HALO:CONV_HALO + Q, :] = xbc_ref[...].astype(F32)
    conv = convb_ref[...]
    for k in range(kw):
        off = CONV_HALO - (kw - 1 - k)
        conv = conv + convw_ref[k:k + 1, :] * ubuf[off:off + Q, :]
    ubuf[0:CONV_HALO, :] = ubuf[Q:Q + CONV_HALO, :]
    xbc = _silu(conv)
    xs = xbc[:, :d_inner]

    dt = _softplus(dt_ref[...] + dtb_ref[...])
    dA = dt * (-jnp.exp(alog_ref[...]))
    row = lax.broadcasted_iota(jnp.int32, (Q, Q), 0)
    col = lax.broadcasted_iota(jnp.int32, (Q, Q), 1)
    causal = row >= col
    tril = causal.astype(BF16)
    hi = dA.astype(BF16)
    mid = (dA - hi.astype(F32)).astype(BF16)
    lo = (dA - hi.astype(F32) - mid.astype(F32)).astype(BF16)
    cum = _dot(tril, hi) + _dot(tril, mid) + _dot(tril, lo)
    cum_t = cum.T
    cum_end = cum[Q - 1:Q, :]

    e2 = e2_ref[...]
    dt_x = _dot(_split2(dt), e2)
    ecum_x = _dot(_split2(jnp.exp(cum)), e2)
    wend_x = _dot(_split2(jnp.exp(cum_end - cum) * dt), e2)
    edec_x = _dot(_split2(jnp.broadcast_to(jnp.exp(cum_end), (8, LANES))), e2)[0:1, :]

    lane = lax.broadcasted_iota(jnp.int32, (Q, LANES), 1)
    first_half = lane < SSM_HEAD_DIM
    xdt = xs * dt_x

    for g in range(n_groups):
        b_g = xbc[:, d_inner + g * d_state:d_inner + (g + 1) * d_state].astype(BF16)
        c_off = d_inner + n_groups * d_state
        c_g = xbc[:, c_off + g * d_state:c_off + (g + 1) * d_state].astype(BF16)
        cb = lax.dot_general(c_g, b_g, (((1,), (1,)), ((), ())), preferred_element_type=F32)
        for p in range(pairs_per_group):
            c0 = g * gw + p * LANES
            x_pair = xdt[:, c0:c0 + LANES]
            acc = None
            for hh in range(2):
                h = g * heads_per_group + 2 * p + hh
                seg = cum[:, h:h + 1] - cum_t[h:h + 1, :]
                decay = jnp.exp(jnp.where(causal, seg, -jnp.inf))
                m = (cb * decay).astype(BF16)
                keep = first_half if hh == 0 else jnp.logical_not(first_half)
                x_h = jnp.where(keep, x_pair, 0.0).astype(BF16)
                part = _dot(m, x_h)
                acc = part if acc is None else acc + part
            ybuf[:, c0:c0 + LANES] = acc
        gs = slice(g * gw, (g + 1) * gw)
        st = state[g]
        y_inter = _dot(c_g, st.astype(BF16)) * ecum_x[:, gs]
        ybuf[:, gs] = ybuf[:, gs] + y_inter
        xw = (xs[:, gs] * wend_x[:, gs]).astype(BF16)
        upd = lax.dot_general(b_g, xw, (((0,), (0,)), ((), ())), preferred_element_type=F32)
        state[g] = st * edec_x[:, gs] + upd

    y = (ybuf[...] + dskip_ref[...] * xs) * _silu(z_ref[...].astype(F32))
    for g in range(n_groups):
        gs = slice(g * gw, (g + 1) * gw)
        ybuf[:, gs] = _rms_scale(y[:, gs])
    yn = (ybuf[...] * gout_ref[...]).astype(BF16)
    o_ref[...] = x_ref[...] + _dot(yn, wout_ref[...])


def _ssd_mixer(z, xbc, dt_raw, x2d, convw, convb, dtb, alog, dskip, gout, wout, e2, batch):
    T, d_inner = z.shape
    D = x2d.shape[1]
    conv_dim = xbc.shape[1]
    n_chunks = T // batch // SSD_CHUNK
    gw = d_inner // SSM_GROUPS
    full = lambda a: pl.BlockSpec(a.shape, lambda b, c: (0, 0))
    rows = lambda n: pl.BlockSpec((SSD_CHUNK, n), lambda b, c: (b * n_chunks + c, 0))
    kernel = functools.partial(_ssd_kernel, d_inner=d_inner, n_groups=SSM_GROUPS, d_state=SSM_STATE)
    return pl.pallas_call(
        kernel,
        grid=(batch, n_chunks),
        in_specs=[rows(d_inner), rows(conv_dim), rows(LANES), rows(D), full(convw), full(convb), full(dtb),
                  full(alog), full(dskip), full(gout), full(wout), full(e2)],
        out_specs=rows(D),
        out_shape=jax.ShapeDtypeStruct((T, D), F32),
        scratch_shapes=[pltpu.VMEM((SSD_CHUNK + CONV_HALO, conv_dim), F32),
                        pltpu.VMEM((SSM_GROUPS, SSM_STATE, gw), F32),
                        pltpu.VMEM((SSD_CHUNK, d_inner), F32)],
        compiler_params=_params("arbitrary", "arbitrary"),
        name="ssd_mixer",
    )(z, xbc, dt_raw, x2d, convw, convb, dtb, alog, dskip, gout, wout, e2)


def _nt_dot(a, b):
    return lax.dot_general(a, b, (((1,), (1,)), ((), ())), preferred_element_type=F32)


def _mla_proj_kernel(h_ref, pos_col_ref, pos_row_ref, invf_row_ref, invf_col_ref, gkv_ref, gq_ref, wdc_ref, wdr_ref,
                     gckv_ref, wuk_ref, wuvt_ref, wcq_ref, wgate_ref, gqn_ref, wuqt_ref, wuqst_ref,
                     qt_ref, k_ref, vt_ref, gate_ref, *, scale):
    hs = _rms_scale(h_ref[...])
    hkv = (hs * gkv_ref[...]).astype(BF16)
    hq = (hs * gq_ref[...]).astype(BF16)

    ang = pos_col_ref[...].astype(F32) * invf_row_ref[...]
    ang_t = invf_col_ref[...] * pos_row_ref[...].astype(F32)

    ckv = (_rms_scale(_dot(hkv, wdc_ref[...])) * gckv_ref[...]).astype(BF16)
    kr2 = _dot(hkv, wdr_ref[...])
    k_rope = kr2[:, :HEAD_LANES] * jnp.cos(ang) + kr2[:, HEAD_LANES:] * jnp.sin(ang)
    vt_ref[0] = _nt_dot(wuvt_ref[...], ckv).astype(BF16)
    k_nope = _dot(ckv, wuk_ref[...])

    gate_ref[...] = _dot(hq, wgate_ref[...]).astype(BF16)
    cq = (_rms_scale(_dot(hq, wcq_ref[...])) * gqn_ref[...]).astype(BF16)
    qm = _nt_dot(wuqt_ref[...], cq)
    qs = _nt_dot(wuqst_ref[...], cq)
    cos_q = jnp.cos(ang_t) * scale
    sin_q = jnp.sin(ang_t) * scale
    for h in range(k_ref.shape[1] // HEAD_LANES):
        sl = slice(h * HEAD_LANES, (h + 1) * HEAD_LANES)
        k_ref[:, sl] = (k_nope[:, sl] + k_rope).astype(BF16)
        qt_ref[0, sl, :] = (qm[sl, :] * cos_q + qs[sl, :] * sin_q).astype(BF16)


def _mla_proj(h1, pos, invf, gkv, gq, wdc, wdr, gckv, wuk, wuvt, wcq, wgate, gqn, wuqt, wuqst, tm, scale):
    T, D = h1.shape
    full = lambda a: pl.BlockSpec(a.shape, lambda i: (0, 0))
    rows = lambda n: pl.BlockSpec((tm, n), lambda i: (i, 0))
    cols = lambda n: pl.BlockSpec((1, n, tm), lambda i: (i, 0, 0))
    hq = wuqt.shape[0]
    hv = wuvt.shape[0]
    hg = wgate.shape[1]
    weights = (invf, invf.reshape(-1, 1), gkv, gq, wdc, wdr, gckv, wuk, wuvt, wcq, wgate, gqn, wuqt, wuqst)
    return pl.pallas_call(
        functools.partial(_mla_proj_kernel, scale=scale),
        grid=(T // tm,),
        in_specs=[rows(D), rows(1), pl.BlockSpec((1, tm), lambda i: (0, i))] + [full(w) for w in weights],
        out_specs=[cols(hq), rows(hq), cols(hv), rows(hg)],
        out_shape=[jax.ShapeDtypeStruct((T // tm, hq, tm), BF16), jax.ShapeDtypeStruct((T, hq), BF16),
                   jax.ShapeDtypeStruct((T // tm, hv, tm), BF16), jax.ShapeDtypeStruct((T, hg), BF16)],
        compiler_params=_params("arbitrary"),
        name="mla_proj",
    )(h1, pos.reshape(T, 1), pos.reshape(1, T), *weights)


def _attn_kernel(qt_ref, k_ref, vt_ref, gate_ref, o_ref, s0_ref, s1_ref, mx_ref, m_ref, l_ref, acc_ref, *, tq):
    qi = pl.program_id(2)
    s_refs = (s0_ref, s1_ref)
    key = lax.broadcasted_iota(jnp.int32, (tq, tq), 0)
    qry = lax.broadcasted_iota(jnp.int32, (tq, tq), 1)

    m_ref[...] = jnp.full(m_ref.shape, -jnp.inf, F32)
    l_ref[...] = jnp.zeros(l_ref.shape, F32)
    acc_ref[...] = jnp.zeros(acc_ref.shape, F32)

    def produce(h, j):
        hs = slice(h * HEAD_LANES, (h + 1) * HEAD_LANES)
        ks = pl.ds(pl.multiple_of(j * tq, tq), tq)
        s = _dot(k_ref[ks, hs], qt_ref[0, hs, :])
        s_refs[h][...] = s
        mx_ref[h] = jnp.max(s, axis=0, keepdims=True)

    def consume(h, j, masked):
        vs = slice(h * V_HEAD, (h + 1) * V_HEAD)
        s = s_refs[h][...]
        if masked:
            s = jnp.where(key <= qry, s, -jnp.inf)
            mx = jnp.max(s, axis=0, keepdims=True)
        else:
            mx = mx_ref[h]
        m_old = m_ref[h]
        m_new = jnp.maximum(m_old, mx)
        alpha = jnp.exp2(m_old - m_new)
        p = jnp.exp2(s - m_new)
        m_ref[h] = m_new
        l_ref[h] = alpha * l_ref[h] + jnp.sum(p, axis=0, keepdims=True)
        acc_ref[h] = alpha * acc_ref[h] + _dot(vt_ref[j, vs, :], p.astype(BF16))

    produce(0, 0)

    def body(j, _):
        produce(1, j)
        consume(0, j, masked=False)
        produce(0, j + 1)
        consume(1, j, masked=False)
        return 0

    lax.fori_loop(0, qi, body, 0)
    produce(1, qi)
    consume(0, qi, masked=True)
    consume(1, qi, masked=True)

    o_t = jnp.concatenate([acc_ref[h] * (1.0 / l_ref[h]) for h in range(2)], axis=0)
    o_ref[...] = (o_t.T * _silu(gate_ref[...].astype(F32))).astype(BF16)


def _mla_attn(qt, k, vt, gate, batch, tq):
    T = k.shape[0]
    S = T // batch
    n_q = S // tq
    n_pairs = vt.shape[1] // LANES
    return pl.pallas_call(
        functools.partial(_attn_kernel, tq=tq),
        grid=(batch, n_pairs, n_q),
        in_specs=[pl.BlockSpec((1, 2 * HEAD_LANES, tq), lambda b, p, i: (b * n_q + i, p, 0)),
                  pl.BlockSpec((S, 2 * HEAD_LANES), lambda b, p, i: (b, p)),
                  pl.BlockSpec((n_q, LANES, tq), lambda b, p, i: (b, p, 0)),
                  pl.BlockSpec((tq, LANES), lambda b, p, i: (b * n_q + i, p))],
        out_specs=pl.BlockSpec((tq, LANES), lambda b, p, i: (b * n_q + i, p)),
        out_shape=jax.ShapeDtypeStruct((T, vt.shape[1]), BF16),
        scratch_shapes=[pltpu.VMEM((tq, tq), F32), pltpu.VMEM((tq, tq), F32),
                        pltpu.VMEM((2, 1, tq), F32), pltpu.VMEM((2, 1, tq), F32), pltpu.VMEM((2, 1, tq), F32),
                        pltpu.VMEM((2, V_HEAD, tq), F32)],
        compiler_params=_params("arbitrary", "arbitrary", "arbitrary"),
        name="mla_attn",
    )(qt, k, vt, gate)


def _mla_out_kernel(o_ref, h_ref, w_ref, g_ref, out_ref):
    h = h_ref[...] + _dot(o_ref[...], w_ref[...])
    out_ref[...] = _rms_scale(h) * g_ref[...]


def _mla_out(o, h1, w, g, tm):
    T, D = h1.shape
    full = lambda a: pl.BlockSpec(a.shape, lambda i: (0, 0))
    rows = lambda n: pl.BlockSpec((tm, n), lambda i: (i, 0))
    return pl.pallas_call(
        _mla_out_kernel,
        grid=(T // tm,),
        in_specs=[rows(o.shape[1]), rows(D), full(w), full(g)],
        out_specs=rows(D),
        out_shape=jax.ShapeDtypeStruct((T, D), F32),
        compiler_params=_params("arbitrary"),
        name="mla_out",
    )(o, h1, w, g)


def _pad_cols(a, n):
    return jnp.pad(a, ((0, 0), (0, n - a.shape[1])))


def _head_slots(w_nope, w_rope):
    k, h = w_nope.shape[0], w_nope.shape[1]
    pad = jnp.zeros((k, h, HEAD_LANES - QK_NOPE - QK_ROPE), w_nope.dtype)
    return jnp.concatenate([w_nope, w_rope, pad], axis=-1).reshape(k, h * HEAD_LANES)


def _rotate_half_cols(w_rope):
    w1, w2 = jnp.split(w_rope, 2, axis=-1)
    return jnp.concatenate([-w2, w1], axis=-1)


def kernel(x, positions, g_pre, ssm_w_in, ssm_conv_w, ssm_conv_b, ssm_dt_bias, ssm_A_log, ssm_D, ssm_g_out,
           ssm_w_out, kv_g_in, kv_w_down, kv_g_latent, kv_w_up, mla_w_in, mla_g_q, mla_w_uq, mla_w_out, g_final):
    B, S, D = x.shape
    T = B * S
    d_inner = ssm_w_out.shape[1]
    n_heads = ssm_dt_bias.shape[1]
    conv_dim = ssm_conv_w.shape[2]
    kv_lora = kv_g_latent.shape[0]
    q_lora = mla_g_q.shape[1]
    H = MLA_HEADS
    assert d_inner == n_heads * SSM_HEAD_DIM and n_heads <= LANES
    assert S % SSD_CHUNK == 0 and ssm_conv_w.shape[1] - 1 <= CONV_HALO
    x2d = x.reshape(T, D)

    w_in = ssm_w_in[0]
    wz = w_in[:, :d_inner].astype(BF16)
    wx = w_in[:, d_inner:d_inner + conv_dim].astype(BF16)
    wdt = _pad_cols(w_in[:, d_inner + conv_dim:], LANES).astype(BF16)
    z, xbc, dt_raw = _ssm_in_proj(x2d, g_pre[0][None, :], wz, wx, wdt, tm=512)

    head_of_channel = jnp.arange(d_inner) // SSM_HEAD_DIM
    expand = (jnp.arange(LANES)[:, None] == head_of_channel[None, :]).astype(BF16)
    e2 = jnp.concatenate([expand, expand], axis=0)
    h1 = _ssd_mixer(
        z, xbc, dt_raw, x2d, ssm_conv_w[0], ssm_conv_b[0][None, :],
        _pad_cols(ssm_dt_bias[0][None, :], LANES), _pad_cols(ssm_A_log[0][None, :], LANES),
        jnp.repeat(ssm_D[0], SSM_HEAD_DIM)[None, :], ssm_g_out[0][None, :], ssm_w_out[0].astype(BF16), e2, batch=B)

    inv = ROPE_BASE ** (-jnp.arange(0, QK_ROPE, 2, dtype=F32) / QK_ROPE)
    invf = jnp.concatenate([jnp.zeros((QK_NOPE,), F32), inv, inv,
                            jnp.zeros((HEAD_LANES - QK_NOPE - QK_ROPE,), F32)])[None, :]
    wdc = kv_w_down[:, :kv_lora].astype(BF16)
    wr = kv_w_down[:, kv_lora:]
    slot1 = lambda w: jnp.pad(w, ((0, 0), (QK_NOPE, HEAD_LANES - QK_NOPE - QK_ROPE)))
    wdr = jnp.concatenate([slot1(wr), slot1(_rotate_half_cols(wr))], axis=1).astype(BF16)
    wup = kv_w_up.reshape(kv_lora, H, QK_NOPE + V_HEAD)
    wuk = _head_slots(wup[:, :, :QK_NOPE], jnp.zeros((kv_lora, H, QK_ROPE), F32)).astype(BF16)
    wuvt = wup[:, :, QK_NOPE:].reshape(kv_lora, H * V_HEAD).T.astype(BF16)
    w_in_b = mla_w_in[0]
    wcq = w_in_b[:, :q_lora].astype(BF16)
    wgate = w_in_b[:, q_lora:].astype(BF16)
    wq = mla_w_uq[0].reshape(q_lora, H, QK_NOPE + QK_ROPE)
    wq_nope, wq_rope = wq[:, :, :QK_NOPE], wq[:, :, QK_NOPE:]
    wuqt = _head_slots(wq_nope, wq_rope).T.astype(BF16)
    wuqst = _head_slots(jnp.zeros_like(wq_nope), _rotate_half_cols(wq_rope)).T.astype(BF16)
    scale = float((QK_NOPE + QK_ROPE) ** -0.5 * math.log2(math.e))

    tq = 512
    qt, k, vt, gate = _mla_proj(
        h1, positions, invf, kv_g_in[None, :], g_pre[1][None, :], wdc, wdr, kv_g_latent[None, :],
        wuk, wuvt, wcq, wgate, mla_g_q[0][None, :], wuqt, wuqst, tm=tq, scale=scale)
    o = _mla_attn(qt, k, vt, gate, batch=B, tq=tq)
    out = _mla_out(o, h1, mla_w_out[0].astype(BF16), g_final[None, :], tm=512)
    return out.reshape(B, S, D)
```

```python
import functools
import math

import jax
import jax.numpy as jnp
from jax import lax
from jax.experimental import pallas as pl
from jax.experimental.pallas import tpu as pltpu

F32 = jnp.float32
BF16 = jnp.bfloat16

SSM_HEAD_DIM = 64
SSM_GROUPS = 4
SSM_STATE = 128
MLA_HEADS = 16
QK_NOPE = 64
QK_ROPE = 32
V_HEAD = 64
ROPE_BASE = 10000.0
EPS = 1e-6

LANES = 128
BF16_SUBLANES = 16
HEAD_LANES = 128
SSD_CHUNK = 256
CONV_HALO = 8
VMEM_LIMIT = 56 * 1024 * 1024


def _params(*sem):
    return pltpu.CompilerParams(dimension_semantics=sem, vmem_limit_bytes=VMEM_LIMIT)


def _silu(x):
    return x * (1.0 / (1.0 + jnp.exp(-x)))


def _softplus(x):
    return jnp.maximum(x, 0.0) + jnp.log1p(jnp.exp(-jnp.abs(x)))


def _rms_scale(x):
    return x * lax.rsqrt(jnp.mean(x * x, axis=-1, keepdims=True) + EPS)


def _dot(a, b):
    return jnp.dot(a, b, preferred_element_type=F32)


def _split2(x):
    hi = x.astype(BF16)
    lo = (x - hi.astype(F32)).astype(BF16)
    return jnp.concatenate([hi, lo], axis=-1)


def _ssm_in_proj_kernel(x_ref, g_ref, wz_ref, wx_ref, wdt_ref, z_ref, xbc_ref, dt_ref):
    hn = (_rms_scale(x_ref[...]) * g_ref[...]).astype(BF16)
    z_ref[...] = _dot(hn, wz_ref[...]).astype(BF16)
    xbc_ref[...] = _dot(hn, wx_ref[...]).astype(BF16)
    dt_ref[...] = _dot(hn, wdt_ref[...])


def _ssm_in_proj(x2d, g, wz, wx, wdt, tm):
    T, D = x2d.shape
    full = lambda a: pl.BlockSpec(a.shape, lambda i: (0, 0))
    rows = lambda n: pl.BlockSpec((tm, n), lambda i: (i, 0))
    return pl.pallas_call(
        _ssm_in_proj_kernel,
        grid=(T // tm,),
        in_specs=[rows(D), full(g), full(wz), full(wx), full(wdt)],
        out_specs=[rows(wz.shape[1]), rows(wx.shape[1]), rows(wdt.shape[1])],
        out_shape=[jax.ShapeDtypeStruct((T, wz.shape[1]), BF16),
                   jax.ShapeDtypeStruct((T, wx.shape[1]), BF16),
                   jax.ShapeDtypeStruct((T, wdt.shape[1]), F32)],
        compiler_params=_params("arbitrary"),
        name="ssm_in_proj",
    )(x2d, g, wz, wx, wdt)


def _ssd_kernel(z_ref, xbc_ref, dt_ref, x_ref, convw_ref, convb_ref, dtb_ref, alog_ref, dskip_ref,
                gout_ref, wout_ref, e2_ref, shift_ref, o_ref, hbuf, state, ybuf, *, d_inner, n_groups, d_state):
    Q = SSD_CHUNK
    gw = d_inner // n_groups
    heads_per_group = gw // SSM_HEAD_DIM
    pairs_per_group = gw // LANES
    kw = convw_ref.shape[0]

    @pl.when(pl.program_id(1) == 0)
    def _():
        hbuf[0:CONV_HALO, :] = jnp.zeros((CONV_HALO, hbuf.shape[1]), F32)
        state[...] = jnp.zeros(state.shape, F32)

    u_bf = xbc_ref[...]
    u = u_bf.astype(F32)
    shifted = _dot(shift_ref[...], u_bf)
    conv = convb_ref[...] + convw_ref[kw - 1:kw, :] * u
    for s in range(1, kw):
        conv = conv + convw_ref[kw - 1 - s:kw - s, :] * shifted[(s - 1) * Q:s * Q, :]
    hbuf[CONV_HALO:2 * CONV_HALO, :] = u[0:CONV_HALO, :]
    head = convb_ref[...]
    for k in range(kw):
        off = CONV_HALO - (kw - 1 - k)
        head = head + convw_ref[k:k + 1, :] * hbuf[off:off + CONV_HALO, :]
    hbuf[0:CONV_HALO, :] = u[Q - CONV_HALO:Q, :]
    conv = jnp.concatenate([head, conv[CONV_HALO:, :]], axis=0)
    xbc = _silu(conv)
    xs = xbc[:, :d_inner]

    dt = _softplus(dt_ref[...] + dtb_ref[...])
    dA = dt * (-jnp.exp(alog_ref[...]))
    row = lax.broadcasted_iota(jnp.int32, (Q, Q), 0)
    col = lax.broadcasted_iota(jnp.int32, (Q, Q), 1)
    causal = row >= col
    tril = causal.astype(BF16)
    hi = dA.astype(BF16)
    mid = (dA - hi.astype(F32)).astype(BF16)
    lo = (dA - hi.astype(F32) - mid.astype(F32)).astype(BF16)
    cum = _dot(tril, hi) + _dot(tril, mid) + _dot(tril, lo)
    cum_t = cum.T
    cum_end = cum[Q - 1:Q, :]

    e2 = e2_ref[...]
    dt_x = _dot(_split2(dt), e2)
    ecum_x = _dot(_split2(jnp.exp(cum)), e2)
    wend_x = _dot(_split2(jnp.exp(cum_end - cum) * dt), e2)
    edec_x = _dot(_split2(jnp.broadcast_to(jnp.exp(cum_end), (8, LANES))), e2)[0:1, :]

    lane = lax.broadcasted_iota(jnp.int32, (Q, LANES), 1)
    first_half = lane < SSM_HEAD_DIM
    xdt = xs * dt_x

    for g in range(n_groups):
        b_g = xbc[:, d_inner + g * d_state:d_inner + (g + 1) * d_state].astype(BF16)
        c_off = d_inner + n_groups * d_state
        c_g = xbc[:, c_off + g * d_state:c_off + (g + 1) * d_state].astype(BF16)
        cb = lax.dot_general(c_g, b_g, (((1,), (1,)), ((), ())), preferred_element_type=F32)
        for p in range(pairs_per_group):
            c0 = g * gw + p * LANES
            x_pair = xdt[:, c0:c0 + LANES]
            acc = None
            for hh in range(2):
                h = g * heads_per_group + 2 * p + hh
                seg = cum[:, h:h + 1] - cum_t[h:h + 1, :]
                decay = jnp.exp(jnp.where(causal, seg, -jnp.inf))
                m = (cb * decay).astype(BF16)
                keep = first_half if hh == 0 else jnp.logical_not(first_half)
                x_h = jnp.where(keep, x_pair, 0.0).astype(BF16)
                part = _dot(m, x_h)
                acc = part if acc is None else acc + part
            ybuf[:, c0:c0 + LANES] = acc
        gs = slice(g * gw, (g + 1) * gw)
        st = state[g]
        y_inter = _dot(c_g, st.astype(BF16)) * ecum_x[:, gs]
        ybuf[:, gs] = ybuf[:, gs] + y_inter
        xw = (xs[:, gs] * wend_x[:, gs]).astype(BF16)
        upd = lax.dot_general(b_g, xw, (((0,), (0,)), ((), ())), preferred_element_type=F32)
        state[g] = st * edec_x[:, gs] + upd

    y = (ybuf[...] + dskip_ref[...] * xs) * _silu(z_ref[...].astype(F32))
    for g in range(n_groups):
        gs = slice(g * gw, (g + 1) * gw)
        ybuf[:, gs] = _rms_scale(y[:, gs])
    yn = (ybuf[...] * gout_ref[...]).astype(BF16)
    o_ref[...] = x_ref[...] + _dot(yn, wout_ref[...])


def _ssd_mixer(z, xbc, dt_raw, x2d, convw, convb, dtb, alog, dskip, gout, wout, e2, shift, batch):
    T, d_inner = z.shape
    D = x2d.shape[1]
    conv_dim = xbc.shape[1]
    n_chunks = T // batch // SSD_CHUNK
    gw = d_inner // SSM_GROUPS
    full = lambda a: pl.BlockSpec(a.shape, lambda b, c: (0, 0))
    rows = lambda n: pl.BlockSpec((SSD_CHUNK, n), lambda b, c: (b * n_chunks + c, 0))
    kernel = functools.partial(_ssd_kernel, d_inner=d_inner, n_groups=SSM_GROUPS, d_state=SSM_STATE)
    return pl.pallas_call(
        kernel,
        grid=(batch, n_chunks),
        in_specs=[rows(d_inner), rows(conv_dim), rows(LANES), rows(D), full(convw), full(convb), full(dtb),
                  full(alog), full(dskip), full(gout), full(wout), full(e2), full(shift)],
        out_specs=rows(D),
        out_shape=jax.ShapeDtypeStruct((T, D), F32),
        scratch_shapes=[pltpu.VMEM((2 * CONV_HALO, conv_dim), F32),
                        pltpu.VMEM((SSM_GROUPS, SSM_STATE, gw), F32),
                        pltpu.VMEM((SSD_CHUNK, d_inner), F32)],
        compiler_params=_params("arbitrary", "arbitrary"),
        name="ssd_mixer",
    )(z, xbc, dt_raw, x2d, convw, convb, dtb, alog, dskip, gout, wout, e2, shift)


def _nt_dot(a, b):
    return lax.dot_general(a, b, (((1,), (1,)), ((), ())), preferred_element_type=F32)


def _mla_proj_kernel(h_ref, pos_col_ref, pos_row_ref, invf_row_ref, invf_col_ref, gkv_ref, gq_ref, wdc_ref, wdr_ref,
                     gckv_ref, wuk_ref, wuvt_ref, wcq_ref, wgate_ref, gqn_ref, wuqt_ref, wuqst_ref,
                     qt_ref, k_ref, vt_ref, gate_ref, *, scale):
    hs = _rms_scale(h_ref[...])
    hkv = (hs * gkv_ref[...]).astype(BF16)
    hq = (hs * gq_ref[...]).astype(BF16)

    ang = pos_col_ref[...].astype(F32) * invf_row_ref[...]
    ang_t = invf_col_ref[...] * pos_row_ref[...].astype(F32)

    ckv = (_rms_scale(_dot(hkv, wdc_ref[...])) * gckv_ref[...]).astype(BF16)
    kr2 = _dot(hkv, wdr_ref[...])
    k_rope = kr2[:, :HEAD_LANES] * jnp.cos(ang) + kr2[:, HEAD_LANES:] * jnp.sin(ang)
    vt_ref[0] = _nt_dot(wuvt_ref[...], ckv).astype(BF16)
    k_nope = _dot(ckv, wuk_ref[...])

    gate_ref[...] = _dot(hq, wgate_ref[...]).astype(BF16)
    cq = (_rms_scale(_dot(hq, wcq_ref[...])) * gqn_ref[...]).astype(BF16)
    qm = _nt_dot(wuqt_ref[...], cq)
    qs = _nt_dot(wuqst_ref[...], cq)
    cos_q = jnp.cos(ang_t) * scale
    sin_q = jnp.sin(ang_t) * scale
    for h in range(k_ref.shape[1] // HEAD_LANES):
        sl = slice(h * HEAD_LANES, (h + 1) * HEAD_LANES)
        k_ref[:, sl] = (k_nope[:, sl] + k_rope).astype(BF16)
        qt_ref[0, sl, :] = (qm[sl, :] * cos_q + qs[sl, :] * sin_q).astype(BF16)


def _mla_proj(h1, pos, invf, gkv, gq, wdc, wdr, gckv, wuk, wuvt, wcq, wgate, gqn, wuqt, wuqst, tm, scale):
    T, D = h1.shape
    full = lambda a: pl.BlockSpec(a.shape, lambda i: (0, 0))
    rows = lambda n: pl.BlockSpec((tm, n), lambda i: (i, 0))
    cols = lambda n: pl.BlockSpec((1, n, tm), lambda i: (i, 0, 0))
    hq = wuqt.shape[0]
    hv = wuvt.shape[0]
    hg = wgate.shape[1]
    weights = (invf, invf.reshape(-1, 1), gkv, gq, wdc, wdr, gckv, wuk, wuvt, wcq, wgate, gqn, wuqt, wuqst)
    return pl.pallas_call(
        functools.partial(_mla_proj_kernel, scale=scale),
        grid=(T // tm,),
        in_specs=[rows(D), rows(1), pl.BlockSpec((1, tm), lambda i: (0, i))] + [full(w) for w in weights],
        out_specs=[cols(hq), rows(hq), cols(hv), rows(hg)],
        out_shape=[jax.ShapeDtypeStruct((T // tm, hq, tm), BF16), jax.ShapeDtypeStruct((T, hq), BF16),
                   jax.ShapeDtypeStruct((T // tm, hv, tm), BF16), jax.ShapeDtypeStruct((T, hg), BF16)],
        compiler_params=_params("arbitrary"),
        name="mla_proj",
    )(h1, pos.reshape(T, 1), pos.reshape(1, T), *weights)


def _attn_kernel(qt_ref, k_ref, vt_ref, gate_ref, o_ref, s0_ref, s1_ref, mx_ref, m_ref, l_ref, acc_ref, *, tq):
    qi = pl.program_id(2)
    s_refs = (s0_ref, s1_ref)
    key = lax.broadcasted_iota(jnp.int32, (tq, tq), 0)
    qry = lax.broadcasted_iota(jnp.int32, (tq, tq), 1)
    ones_rows = jnp.ones((BF16_SUBLANES, tq), BF16)

    m_ref[...] = jnp.full(m_ref.shape, -jnp.inf, F32)
    l_ref[...] = jnp.zeros(l_ref.shape, F32)
    acc_ref[...] = jnp.zeros(acc_ref.shape, F32)

    def produce(h, j):
        hs = slice(h * HEAD_LANES, (h + 1) * HEAD_LANES)
        ks = pl.ds(pl.multiple_of(j * tq, tq), tq)
        s = _dot(k_ref[ks, hs], qt_ref[0, hs, :])
        s_refs[h][...] = s
        mx_ref[h] = jnp.max(s, axis=0, keepdims=True)

    def consume(h, j, masked):
        vs = slice(h * V_HEAD, (h + 1) * V_HEAD)
        s = s_refs[h][...]
        if masked:
            s = jnp.where(key <= qry, s, -jnp.inf)
            mx = jnp.max(s, axis=0, keepdims=True)
        else:
            mx = mx_ref[h]
        m_old = m_ref[h]
        m_new = jnp.maximum(m_old, mx)
        alpha = jnp.exp2(m_old - m_new)
        p = jnp.exp2(s - m_new)
        m_ref[h] = m_new
        pv = _dot(jnp.concatenate([vt_ref[j, vs, :], ones_rows], axis=0), p.astype(BF16))
        acc_ref[h] = alpha * acc_ref[h] + pv[:V_HEAD, :]
        l_ref[h] = alpha * l_ref[h] + pv[V_HEAD:V_HEAD + 1, :]

    produce(0, 0)

    def body(j, _):
        produce(1, j)
        consume(0, j, masked=False)
        produce(0, j + 1)
        consume(1, j, masked=False)
        return 0

    lax.fori_loop(0, qi, body, 0)
    produce(1, qi)
    consume(0, qi, masked=True)
    consume(1, qi, masked=True)

    o_t = jnp.concatenate([acc_ref[h] * (1.0 / l_ref[h]) for h in range(2)], axis=0)
    o_ref[...] = (o_t.T * _silu(gate_ref[...].astype(F32))).astype(BF16)


def _mla_attn(qt, k, vt, gate, batch, tq):
    T = k.shape[0]
    S = T // batch
    n_q = S // tq
    n_pairs = vt.shape[1] // LANES
    return pl.pallas_call(
        functools.partial(_attn_kernel, tq=tq),
        grid=(batch, n_pairs, n_q),
        in_specs=[pl.BlockSpec((1, 2 * HEAD_LANES, tq), lambda b, p, i: (b * n_q + i, p, 0)),
                  pl.BlockSpec((S, 2 * HEAD_LANES), lambda b, p, i: (b, p)),
                  pl.BlockSpec((n_q, LANES, tq), lambda b, p, i: (b, p, 0)),
                  pl.BlockSpec((tq, LANES), lambda b, p, i: (b * n_q + i, p))],
        out_specs=pl.BlockSpec((tq, LANES), lambda b, p, i: (b * n_q + i, p)),
        out_shape=jax.ShapeDtypeStruct((T, vt.shape[1]), BF16),
        scratch_shapes=[pltpu.VMEM((tq, tq), F32), pltpu.VMEM((tq, tq), F32),
                        pltpu.VMEM((2, 1, tq), F32), pltpu.VMEM((2, 1, tq), F32), pltpu.VMEM((2, 1, tq), F32),
                        pltpu.VMEM((2, V_HEAD, tq), F32)],
        compiler_params=_params("arbitrary", "arbitrary", "arbitrary"),
        name="mla_attn",
    )(qt, k, vt, gate)


def _mla_out_kernel(o_ref, h_ref, w_ref, g_ref, out_ref):
    h = h_ref[...] + _dot(o_ref[...], w_ref[...])
    out_ref[...] = _rms_scale(h) * g_ref[...]


def _mla_out(o, h1, w, g, tm):
    T, D = h1.shape
    full = lambda a: pl.BlockSpec(a.shape, lambda i: (0, 0))
    rows = lambda n: pl.BlockSpec((tm, n), lambda i: (i, 0))
    return pl.pallas_call(
        _mla_out_kernel,
        grid=(T // tm,),
        in_specs=[rows(o.shape[1]), rows(D), full(w), full(g)],
        out_specs=rows(D),
        out_shape=jax.ShapeDtypeStruct((T, D), F32),
        compiler_params=_params("arbitrary"),
        name="mla_out",
    )(o, h1, w, g)


def _pad_cols(a, n):
    return jnp.pad(a, ((0, 0), (0, n - a.shape[1])))


def _head_slots(w_nope, w_rope):
    k, h = w_nope.shape[0], w_nope.shape[1]
    pad = jnp.zeros((k, h, HEAD_LANES - QK_NOPE - QK_ROPE), w_nope.dtype)
    return jnp.concatenate([w_nope, w_rope, pad], axis=-1).reshape(k, h * HEAD_LANES)


def _rotate_half_cols(w_rope):
    w1, w2 = jnp.split(w_rope, 2, axis=-1)
    return jnp.concatenate([-w2, w1], axis=-1)


def kernel(x, positions, g_pre, ssm_w_in, ssm_conv_w, ssm_conv_b, ssm_dt_bias, ssm_A_log, ssm_D, ssm_g_out,
           ssm_w_out, kv_g_in, kv_w_down, kv_g_latent, kv_w_up, mla_w_in, mla_g_q, mla_w_uq, mla_w_out, g_final):
    B, S, D = x.shape
    T = B * S
    d_inner = ssm_w_out.shape[1]
    n_heads = ssm_dt_bias.shape[1]
    conv_dim = ssm_conv_w.shape[2]
    kv_lora = kv_g_latent.shape[0]
    q_lora = mla_g_q.shape[1]
    H = MLA_HEADS
    assert d_inner == n_heads * SSM_HEAD_DIM and n_heads <= LANES
    assert S % SSD_CHUNK == 0 and ssm_conv_w.shape[1] - 1 <= CONV_HALO
    x2d = x.reshape(T, D)

    w_in = ssm_w_in[0]
    wz = w_in[:, :d_inner].astype(BF16)
    wx = w_in[:, d_inner:d_inner + conv_dim].astype(BF16)
    wdt = _pad_cols(w_in[:, d_inner + conv_dim:], LANES).astype(BF16)
    z, xbc, dt_raw = _ssm_in_proj(x2d, g_pre[0][None, :], wz, wx, wdt, tm=512)

    head_of_channel = jnp.arange(d_inner) // SSM_HEAD_DIM
    expand = (jnp.arange(LANES)[:, None] == head_of_channel[None, :]).astype(BF16)
    e2 = jnp.concatenate([expand, expand], axis=0)
    t_idx = jnp.arange(SSD_CHUNK)
    kw = ssm_conv_w.shape[1]
    shift = jnp.concatenate([(t_idx[:, None] - s == t_idx[None, :]) for s in range(1, kw)], axis=0).astype(BF16)
    h1 = _ssd_mixer(
        z, xbc, dt_raw, x2d, ssm_conv_w[0], ssm_conv_b[0][None, :],
        _pad_cols(ssm_dt_bias[0][None, :], LANES), _pad_cols(ssm_A_log[0][None, :], LANES),
        jnp.repeat(ssm_D[0], SSM_HEAD_DIM)[None, :], ssm_g_out[0][None, :], ssm_w_out[0].astype(BF16), e2, shift,
        batch=B)

    inv = ROPE_BASE ** (-jnp.arange(0, QK_ROPE, 2, dtype=F32) / QK_ROPE)
    invf = jnp.concatenate([jnp.zeros((QK_NOPE,), F32), inv, inv,
                            jnp.zeros((HEAD_LANES - QK_NOPE - QK_ROPE,), F32)])[None, :]
    wdc = kv_w_down[:, :kv_lora].astype(BF16)
    wr = kv_w_down[:, kv_lora:]
    slot1 = lambda w: jnp.pad(w, ((0, 0), (QK_NOPE, HEAD_LANES - QK_NOPE - QK_ROPE)))
    wdr = jnp.concatenate([slot1(wr), slot1(_rotate_half_cols(wr))], axis=1).astype(BF16)
    wup = kv_w_up.reshape(kv_lora, H, QK_NOPE + V_HEAD)
    wuk = _head_slots(wup[:, :, :QK_NOPE], jnp.zeros((kv_lora, H, QK_ROPE), F32)).astype(BF16)
    wuvt = wup[:, :, QK_NOPE:].reshape(kv_lora, H * V_HEAD).T.astype(BF16)
    w_in_b = mla_w_in[0]
    wcq = w_in_b[:, :q_lora].astype(BF16)
    wgate = w_in_b[:, q_lora:].astype(BF16)
    wq = mla_w_uq[0].reshape(q_lora, H, QK_NOPE + QK_ROPE)
    wq_nope, wq_rope = wq[:, :, :QK_NOPE], wq[:, :, QK_NOPE:]
    wuqt = _head_slots(wq_nope, wq_rope).T.astype(BF16)
    wuqst = _head_slots(jnp.zeros_like(wq_nope), _rotate_half_cols(wq_rope)).T.astype(BF16)
    scale = float((QK_NOPE + QK_ROPE) ** -0.5 * math.log2(math.e))

    tq = 512
    qt, k, vt, gate = _mla_proj(
        h1, positions, invf, kv_g_in[None, :], g_pre[1][None, :], wdc, wdr, kv_g_latent[None, :],
        wuk, wuvt, wcq, wgate, mla_g_q[0][None, :], wuqt, wuqst, tm=tq, scale=scale)
    o = _mla_attn(qt, k, vt, gate, batch=B, tq=tq)
    out = _mla_out(o, h1, mla_w_out[0].astype(BF16), g_final[None, :], tm=512)
    return out.reshape(B, S, D)
```

```python
import functools
import math

import jax
import jax.numpy as jnp
from jax import lax
from jax.experimental import pallas as pl
from jax.experimental.pallas import tpu as pltpu

F32 = jnp.float32
BF16 = jnp.bfloat16

SSM_HEAD_DIM = 64
SSM_GROUPS = 4
SSM_STATE = 128
MLA_HEADS = 16
QK_NOPE = 64
QK_ROPE = 32
V_HEAD = 64
ROPE_BASE = 10000.0
EPS = 1e-6

LANES = 128
BF16_SUBLANES = 16
HEAD_LANES = 128
SSD_CHUNK = 256
CONV_HALO = 8
VMEM_LIMIT = 56 * 1024 * 1024


def _params(*sem):
    return pltpu.CompilerParams(dimension_semantics=sem, vmem_limit_bytes=VMEM_LIMIT)


def _silu(x):
    return x * (1.0 / (1.0 + jnp.exp(-x)))


def _softplus(x):
    return jnp.maximum(x, 0.0) + jnp.log1p(jnp.exp(-jnp.abs(x)))


def _rms_scale(x):
    return x * lax.rsqrt(jnp.mean(x * x, axis=-1, keepdims=True) + EPS)


def _dot(a, b):
    return jnp.dot(a, b, preferred_element_type=F32)


def _split2(x):
    hi = x.astype(BF16)
    lo = (x - hi.astype(F32)).astype(BF16)
    return jnp.concatenate([hi, lo], axis=-1)


def _ssm_in_proj_kernel(x_ref, g_ref, wz_ref, wx_ref, wdt_ref, z_ref, xbc_ref, dt_ref):
    hn = (_rms_scale(x_ref[...]) * g_ref[...]).astype(BF16)
    z_ref[...] = _dot(hn, wz_ref[...]).astype(BF16)
    xbc_ref[...] = _dot(hn, wx_ref[...]).astype(BF16)
    dt_ref[...] = _dot(hn, wdt_ref[...])


def _ssm_in_proj(x2d, g, wz, wx, wdt, tm):
    T, D = x2d.shape
    full = lambda a: pl.BlockSpec(a.shape, lambda i: (0, 0))
    rows = lambda n: pl.BlockSpec((tm, n), lambda i: (i, 0))
    return pl.pallas_call(
        _ssm_in_proj_kernel,
        grid=(T // tm,),
        in_specs=[rows(D), full(g), full(wz), full(wx), full(wdt)],
        out_specs=[rows(wz.shape[1]), rows(wx.shape[1]), rows(wdt.shape[1])],
        out_shape=[jax.ShapeDtypeStruct((T, wz.shape[1]), BF16),
                   jax.ShapeDtypeStruct((T, wx.shape[1]), BF16),
                   jax.ShapeDtypeStruct((T, wdt.shape[1]), F32)],
        compiler_params=_params("arbitrary"),
        name="ssm_in_proj",
    )(x2d, g, wz, wx, wdt)


def _ssd_kernel(z_ref, xbc_ref, dt_ref, x_ref, convw_ref, convb_ref, dtb_ref, alog_ref, dskip_ref,
                gout_ref, wout_ref, e2_ref, shift_ref, o_ref, hbuf, state, ybuf, *, d_inner, n_groups, d_state):
    Q = SSD_CHUNK
    gw = d_inner // n_groups
    heads_per_group = gw // SSM_HEAD_DIM
    pairs_per_group = gw // LANES
    kw = convw_ref.shape[0]

    @pl.when(pl.program_id(1) == 0)
    def _():
        hbuf[0:CONV_HALO, :] = jnp.zeros((CONV_HALO, hbuf.shape[1]), F32)
        state[...] = jnp.zeros(state.shape, F32)

    u_bf = xbc_ref[...]
    u = u_bf.astype(F32)
    shifted = _dot(shift_ref[...], u_bf)
    conv = convb_ref[...] + convw_ref[kw - 1:kw, :] * u
    for s in range(1, kw):
        conv = conv + convw_ref[kw - 1 - s:kw - s, :] * shifted[(s - 1) * Q:s * Q, :]
    hbuf[CONV_HALO:2 * CONV_HALO, :] = u[0:CONV_HALO, :]
    head = convb_ref[...]
    for k in range(kw):
        off = CONV_HALO - (kw - 1 - k)
        head = head + convw_ref[k:k + 1, :] * hbuf[off:off + CONV_HALO, :]
    hbuf[0:CONV_HALO, :] = u[Q - CONV_HALO:Q, :]
    conv = jnp.concatenate([head, conv[CONV_HALO:, :]], axis=0)
    xbc = _silu(conv)
    xs = xbc[:, :d_inner]

    dt = _softplus(dt_ref[...] + dtb_ref[...])
    dA = dt * (-jnp.exp(alog_ref[...]))
    row = lax.broadcasted_iota(jnp.int32, (Q, Q), 0)
    col = lax.broadcasted_iota(jnp.int32, (Q, Q), 1)
    causal = row >= col
    tril = causal.astype(BF16)
    hi = dA.astype(BF16)
    mid = (dA - hi.astype(F32)).astype(BF16)
    lo = (dA - hi.astype(F32) - mid.astype(F32)).astype(BF16)
    cum = _dot(tril, hi) + _dot(tril, mid) + _dot(tril, lo)
    cum_t = cum.T
    cum_end = cum[Q - 1:Q, :]

    e2 = e2_ref[...]
    dt_x = _dot(_split2(dt), e2)
    ecum_x = _dot(_split2(jnp.exp(cum)), e2)
    wend_x = _dot(_split2(jnp.exp(cum_end - cum) * dt), e2)
    edec_x = _dot(_split2(jnp.broadcast_to(jnp.exp(cum_end), (8, LANES))), e2)[0:1, :]

    lane = lax.broadcasted_iota(jnp.int32, (Q, LANES), 1)
    first_half = lane < SSM_HEAD_DIM
    xdt = xs * dt_x

    for g in range(n_groups):
        b_g = xbc[:, d_inner + g * d_state:d_inner + (g + 1) * d_state].astype(BF16)
        c_off = d_inner + n_groups * d_state
        c_g = xbc[:, c_off + g * d_state:c_off + (g + 1) * d_state].astype(BF16)
        cb = lax.dot_general(c_g, b_g, (((1,), (1,)), ((), ())), preferred_element_type=F32)
        for p in range(pairs_per_group):
            c0 = g * gw + p * LANES
            x_pair = xdt[:, c0:c0 + LANES]
            acc = None
            for hh in range(2):
                h = g * heads_per_group + 2 * p + hh
                seg = cum[:, h:h + 1] - cum_t[h:h + 1, :]
                decay = jnp.exp(jnp.where(causal, seg, -jnp.inf))
                m = (cb * decay).astype(BF16)
                keep = first_half if hh == 0 else jnp.logical_not(first_half)
                x_h = jnp.where(keep, x_pair, 0.0).astype(BF16)
                part = _dot(m, x_h)
                acc = part if acc is None else acc + part
            ybuf[:, c0:c0 + LANES] = acc
        gs = slice(g * gw, (g + 1) * gw)
        st = state[g]
        y_inter = _dot(c_g, st.astype(BF16)) * ecum_x[:, gs]
        ybuf[:, gs] = ybuf[:, gs] + y_inter
        xw = (xs[:, gs] * wend_x[:, gs]).astype(BF16)
        upd = lax.dot_general(b_g, xw, (((0,), (0,)), ((), ())), preferred_element_type=F32)
        state[g] = st * edec_x[:, gs] + upd

    y = (ybuf[...] + dskip_ref[...] * xs) * _silu(z_ref[...].astype(F32))
    for g in range(n_groups):
        gs = slice(g * gw, (g + 1) * gw)
        ybuf[:, gs] = _rms_scale(y[:, gs])
    yn = (ybuf[...] * gout_ref[...]).astype(BF16)
    o_ref[...] = x_ref[...] + _dot(yn, wout_ref[...])


def _ssd_mixer(z, xbc, dt_raw, x2d, convw, convb, dtb, alog, dskip, gout, wout, e2, shift, batch):
    T, d_inner = z.shape
    D = x2d.shape[1]
    conv_dim = xbc.shape[1]
    n_chunks = T // batch // SSD_CHUNK
    gw = d_inner // SSM_GROUPS
    full = lambda a: pl.BlockSpec(a.shape, lambda b, c: (0, 0))
    rows = lambda n: pl.BlockSpec((SSD_CHUNK, n), lambda b, c: (b * n_chunks + c, 0))
    kernel = functools.partial(_ssd_kernel, d_inner=d_inner, n_groups=SSM_GROUPS, d_state=SSM_STATE)
    return pl.pallas_call(
        kernel,
        grid=(batch, n_chunks),
        in_specs=[rows(d_inner), rows(conv_dim), rows(LANES), rows(D), full(convw), full(convb), full(dtb),
                  full(alog), full(dskip), full(gout), full(wout), full(e2), full(shift)],
        out_specs=rows(D),
        out_shape=jax.ShapeDtypeStruct((T, D), F32),
        scratch_shapes=[pltpu.VMEM((2 * CONV_HALO, conv_dim), F32),
                        pltpu.VMEM((SSM_GROUPS, SSM_STATE, gw), F32),
                        pltpu.VMEM((SSD_CHUNK, d_inner), F32)],
        compiler_params=_params("arbitrary", "arbitrary"),
        name="ssd_mixer",
    )(z, xbc, dt_raw, x2d, convw, convb, dtb, alog, dskip, gout, wout, e2, shift)


def _nt_dot(a, b):
    return lax.dot_general(a, b, (((1,), (1,)), ((), ())), preferred_element_type=F32)


def _mla_proj_kernel(h_ref, pos_ref, invf_ref, gkv_ref, gq_ref, wdc_ref, wdr_ref,
                     gckv_ref, wuk_ref, wuvt_ref, wcq_ref, wgate_ref, gqn_ref, wuqt_ref,
                     qt_ref, k_ref, vt_ref, gate_ref, *, scale):
    tm = h_ref.shape[0]
    half = QK_ROPE // 2
    hs = _rms_scale(h_ref[...])
    hkv = (hs * gkv_ref[...]).astype(BF16)
    hq = (hs * gq_ref[...]).astype(BF16)

    ang = invf_ref[...] * pos_ref[...].astype(F32)
    cos = jnp.cos(ang)
    sin = jnp.sin(ang)

    ones = jnp.ones((QK_NOPE, tm), F32)
    tail = HEAD_LANES - QK_NOPE - QK_ROPE
    cos_k = jnp.concatenate([ones, cos, cos, ones[:tail]], axis=0).T
    sin_k = jnp.concatenate([0.0 * ones, sin, sin, 0.0 * ones[:tail]], axis=0).T
    ckv = (_rms_scale(_dot(hkv, wdc_ref[...])) * gckv_ref[...]).astype(BF16)
    kr2 = _dot(hkv, wdr_ref[...])
    k_rope = kr2[:, :HEAD_LANES] * cos_k + kr2[:, HEAD_LANES:] * sin_k
    vt_ref[0] = _nt_dot(wuvt_ref[...], ckv).astype(BF16)
    k_nope = _dot(ckv, wuk_ref[...])

    gate_ref[...] = _dot(hq, wgate_ref[...]).astype(BF16)
    cq = (_rms_scale(_dot(hq, wcq_ref[...])) * gqn_ref[...]).astype(BF16)
    qm = _nt_dot(wuqt_ref[...], cq)
    cos_q = cos * scale
    sin_q = sin * scale
    for h in range(k_ref.shape[1] // HEAD_LANES):
        r0 = h * HEAD_LANES
        r1 = r0 + QK_NOPE
        r2 = r1 + half
        r3 = r2 + half
        k_ref[:, r0:r0 + HEAD_LANES] = (k_nope[:, r0:r0 + HEAD_LANES] + k_rope).astype(BF16)
        x1 = qm[r1:r2, :]
        x2 = qm[r2:r3, :]
        qt_ref[0, r0:r1, :] = (qm[r0:r1, :] * scale).astype(BF16)
        qt_ref[0, r1:r2, :] = (x1 * cos_q - x2 * sin_q).astype(BF16)
        qt_ref[0, r2:r3, :] = (x1 * sin_q + x2 * cos_q).astype(BF16)
        qt_ref[0, r3:r0 + HEAD_LANES, :] = qm[r3:r0 + HEAD_LANES, :].astype(BF16)


def _mla_proj(h1, pos, invf, gkv, gq, wdc, wdr, gckv, wuk, wuvt, wcq, wgate, gqn, wuqt, tm, scale):
    T, D = h1.shape
    full = lambda a: pl.BlockSpec(a.shape, lambda i: (0, 0))
    rows = lambda n: pl.BlockSpec((tm, n), lambda i: (i, 0))
    cols = lambda n: pl.BlockSpec((1, n, tm), lambda i: (i, 0, 0))
    hq = wuqt.shape[0]
    hv = wuvt.shape[0]
    hg = wgate.shape[1]
    weights = (invf, gkv, gq, wdc, wdr, gckv, wuk, wuvt, wcq, wgate, gqn, wuqt)
    return pl.pallas_call(
        functools.partial(_mla_proj_kernel, scale=scale),
        grid=(T // tm,),
        in_specs=[rows(D), pl.BlockSpec((1, tm), lambda i: (0, i))] + [full(w) for w in weights],
        out_specs=[cols(hq), rows(hq), cols(hv), rows(hg)],
        out_shape=[jax.ShapeDtypeStruct((T // tm, hq, tm), BF16), jax.ShapeDtypeStruct((T, hq), BF16),
                   jax.ShapeDtypeStruct((T // tm, hv, tm), BF16), jax.ShapeDtypeStruct((T, hg), BF16)],
        compiler_params=_params("arbitrary"),
        name="mla_proj",
    )(h1, pos.reshape(1, T), *weights)


ATTN_Q_SPLIT = 2


def _attn_kernel(qt_ref, k_ref, vt_ref, gate_ref, o_ref, *scratch, tq):
    n_units = 2 * ATTN_Q_SPLIT
    s_refs = scratch[:n_units]
    p_refs = scratch[n_units:2 * n_units]
    mx_ref, m_ref, alpha_ref, l_ref, acc_ref = scratch[2 * n_units:]
    qc = tq // ATTN_Q_SPLIT
    qi = pl.program_id(2)
    key = lax.broadcasted_iota(jnp.int32, (tq, qc), 0)
    qry = lax.broadcasted_iota(jnp.int32, (tq, qc), 1)
    ones_rows = jnp.ones((BF16_SUBLANES, tq), BF16)
    last = n_units - 1

    m_ref[...] = jnp.full(m_ref.shape, -jnp.inf, F32)
    l_ref[...] = jnp.zeros(l_ref.shape, F32)
    acc_ref[...] = jnp.zeros(acc_ref.shape, F32)
    p_refs[last][...] = jnp.zeros(p_refs[last].shape, BF16)
    alpha_ref[...] = jnp.ones(alpha_ref.shape, F32)

    def scores(u, j):
        h, c = divmod(u, ATTN_Q_SPLIT)
        hs = slice(h * HEAD_LANES, (h + 1) * HEAD_LANES)
        ks = pl.ds(pl.multiple_of(j * tq, tq), tq)
        s = _dot(k_ref[ks, hs], qt_ref[0, hs, c * qc:(c + 1) * qc])
        s_refs[u][...] = s
        mx_ref[u] = jnp.max(s, axis=0, keepdims=True)

    def probs(u, masked):
        c = u % ATTN_Q_SPLIT
        s = s_refs[u][...]
        if masked:
            s = jnp.where(key <= qry + c * qc, s, -jnp.inf)
            mx = jnp.max(s, axis=0, keepdims=True)
        else:
            mx = mx_ref[u]
        m_old = m_ref[u]
        m_new = jnp.maximum(m_old, mx)
        alpha_ref[u] = jnp.exp2(m_old - m_new)
        m_ref[u] = m_new
        p_refs[u][...] = jnp.exp2(s - m_new).astype(BF16)

    def values(u, j):
        h = u // ATTN_Q_SPLIT
        vs = slice(h * V_HEAD, (h + 1) * V_HEAD)
        pv = _dot(jnp.concatenate([vt_ref[j, vs, :], ones_rows], axis=0), p_refs[u][...])
        alpha = alpha_ref[u]
        acc_ref[u] = alpha * acc_ref[u] + pv[:V_HEAD, :]
        l_ref[u] = alpha * l_ref[u] + pv[V_HEAD:V_HEAD + 1, :]

    for u in range(last):
        scores(u, 0)

    def body(j, _):
        values(last, jnp.maximum(j - 1, 0))
        scores(last, j)
        for u in range(last):
            probs(u, masked=False)
            values(u, j)
            scores(u, j + 1)
        probs(last, masked=False)
        return 0

    lax.fori_loop(0, qi, body, 0)
    values(last, jnp.maximum(qi - 1, 0))
    scores(last, qi)
    for u in range(n_units):
        probs(u, masked=True)
        values(u, qi)

    o_t = jnp.concatenate(
        [jnp.concatenate([acc_ref[h * ATTN_Q_SPLIT + c] * (1.0 / l_ref[h * ATTN_Q_SPLIT + c])
                          for c in range(ATTN_Q_SPLIT)], axis=1) for h in range(2)], axis=0)
    o_ref[...] = (o_t.T * _silu(gate_ref[...].astype(F32))).astype(BF16)


def _mla_attn(qt, k, vt, gate, batch, tq):
    T = k.shape[0]
    S = T // batch
    n_q = S // tq
    n_pairs = vt.shape[1] // LANES
    n_units = 2 * ATTN_Q_SPLIT
    qc = tq // ATTN_Q_SPLIT
    stat = pltpu.VMEM((n_units, 1, qc), F32)
    return pl.pallas_call(
        functools.partial(_attn_kernel, tq=tq),
        grid=(batch, n_pairs, n_q),
        in_specs=[pl.BlockSpec((1, 2 * HEAD_LANES, tq), lambda b, p, i: (b * n_q + i, p, 0)),
                  pl.BlockSpec((S, 2 * HEAD_LANES), lambda b, p, i: (b, p)),
                  pl.BlockSpec((n_q, LANES, tq), lambda b, p, i: (b, p, 0)),
                  pl.BlockSpec((tq, LANES), lambda b, p, i: (b * n_q + i, p))],
        out_specs=pl.BlockSpec((tq, LANES), lambda b, p, i: (b * n_q + i, p)),
        out_shape=jax.ShapeDtypeStruct((T, vt.shape[1]), BF16),
        scratch_shapes=([pltpu.VMEM((tq, qc), F32)] * n_units + [pltpu.VMEM((tq, qc), BF16)] * n_units
                        + [stat, stat, stat, stat, pltpu.VMEM((n_units, V_HEAD, qc), F32)]),
        compiler_params=_params("arbitrary", "arbitrary", "arbitrary"),
        name="mla_attn",
    )(qt, k, vt, gate)


def _mla_out_kernel(o_ref, h_ref, w_ref, g_ref, out_ref):
    h = h_ref[...] + _dot(o_ref[...], w_ref[...])
    out_ref[...] = _rms_scale(h) * g_ref[...]


def _mla_out(o, h1, w, g, tm):
    T, D = h1.shape
    full = lambda a: pl.BlockSpec(a.shape, lambda i: (0, 0))
    rows = lambda n: pl.BlockSpec((tm, n), lambda i: (i, 0))
    return pl.pallas_call(
        _mla_out_kernel,
        grid=(T // tm,),
        in_specs=[rows(o.shape[1]), rows(D), full(w), full(g)],
        out_specs=rows(D),
        out_shape=jax.ShapeDtypeStruct((T, D), F32),
        compiler_params=_params("arbitrary"),
        name="mla_out",
    )(o, h1, w, g)


def _pad_cols(a, n):
    return jnp.pad(a, ((0, 0), (0, n - a.shape[1])))


def _head_slots(w_nope, w_rope):
    k, h = w_nope.shape[0], w_nope.shape[1]
    pad = jnp.zeros((k, h, HEAD_LANES - QK_NOPE - QK_ROPE), w_nope.dtype)
    return jnp.concatenate([w_nope, w_rope, pad], axis=-1).reshape(k, h * HEAD_LANES)


def _rotate_half_cols(w_rope):
    w1, w2 = jnp.split(w_rope, 2, axis=-1)
    return jnp.concatenate([-w2, w1], axis=-1)


def kernel(x, positions, g_pre, ssm_w_in, ssm_conv_w, ssm_conv_b, ssm_dt_bias, ssm_A_log, ssm_D, ssm_g_out,
           ssm_w_out, kv_g_in, kv_w_down, kv_g_latent, kv_w_up, mla_w_in, mla_g_q, mla_w_uq, mla_w_out, g_final):
    B, S, D = x.shape
    T = B * S
    d_inner = ssm_w_out.shape[1]
    n_heads = ssm_dt_bias.shape[1]
    conv_dim = ssm_conv_w.shape[2]
    kv_lora = kv_g_latent.shape[0]
    q_lora = mla_g_q.shape[1]
    H = MLA_HEADS
    assert d_inner == n_heads * SSM_HEAD_DIM and n_heads <= LANES
    assert S % SSD_CHUNK == 0 and ssm_conv_w.shape[1] - 1 <= CONV_HALO
    x2d = x.reshape(T, D)

    w_in = ssm_w_in[0]
    wz = w_in[:, :d_inner].astype(BF16)
    wx = w_in[:, d_inner:d_inner + conv_dim].astype(BF16)
    wdt = _pad_cols(w_in[:, d_inner + conv_dim:], LANES).astype(BF16)
    z, xbc, dt_raw = _ssm_in_proj(x2d, g_pre[0][None, :], wz, wx, wdt, tm=512)

    head_of_channel = jnp.arange(d_inner) // SSM_HEAD_DIM
    expand = (jnp.arange(LANES)[:, None] == head_of_channel[None, :]).astype(BF16)
    e2 = jnp.concatenate([expand, expand], axis=0)
    t_idx = jnp.arange(SSD_CHUNK)
    kw = ssm_conv_w.shape[1]
    shift = jnp.concatenate([(t_idx[:, None] - s == t_idx[None, :]) for s in range(1, kw)], axis=0).astype(BF16)
    h1 = _ssd_mixer(
        z, xbc, dt_raw, x2d, ssm_conv_w[0], ssm_conv_b[0][None, :],
        _pad_cols(ssm_dt_bias[0][None, :], LANES), _pad_cols(ssm_A_log[0][None, :], LANES),
        jnp.repeat(ssm_D[0], SSM_HEAD_DIM)[None, :], ssm_g_out[0][None, :], ssm_w_out[0].astype(BF16), e2, shift,
        batch=B)

    invf = (ROPE_BASE ** (-jnp.arange(0, QK_ROPE, 2, dtype=F32) / QK_ROPE))[:, None]
    wdc = kv_w_down[:, :kv_lora].astype(BF16)
    wr = kv_w_down[:, kv_lora:]
    slot1 = lambda w: jnp.pad(w, ((0, 0), (QK_NOPE, HEAD_LANES - QK_NOPE - QK_ROPE)))
    wdr = jnp.concatenate([slot1(wr), slot1(_rotate_half_cols(wr))], axis=1).astype(BF16)
    wup = kv_w_up.reshape(kv_lora, H, QK_NOPE + V_HEAD)
    wuk = _head_slots(wup[:, :, :QK_NOPE], jnp.zeros((kv_lora, H, QK_ROPE), F32)).astype(BF16)
    wuvt = wup[:, :, QK_NOPE:].reshape(kv_lora, H * V_HEAD).T.astype(BF16)
    w_in_b = mla_w_in[0]
    wcq = w_in_b[:, :q_lora].astype(BF16)
    wgate = w_in_b[:, q_lora:].astype(BF16)
    wq = mla_w_uq[0].reshape(q_lora, H, QK_NOPE + QK_ROPE)
    wq_nope, wq_rope = wq[:, :, :QK_NOPE], wq[:, :, QK_NOPE:]
    wuqt = _head_slots(wq_nope, wq_rope).T.astype(BF16)
    scale = float((QK_NOPE + QK_ROPE) ** -0.5 * math.log2(math.e))

    tq = 512
    qt, k, vt, gate = _mla_proj(
        h1, positions, invf, kv_g_in[None, :], g_pre[1][None, :], wdc, wdr, kv_g_latent[None, :],
        wuk, wuvt, wcq, wgate, mla_g_q[0][None, :], wuqt, tm=tq, scale=scale)
    o = _mla_attn(qt, k, vt, gate, batch=B, tq=tq)
    out = _mla_out(o, h1, mla_w_out[0].astype(BF16), g_final[None, :], tm=512)
    return out.reshape(B, S, D)
```

```python
import functools
import math

import jax
import jax.numpy as jnp
from jax import lax
from jax.experimental import pallas as pl
from jax.experimental.pallas import tpu as pltpu

F32 = jnp.float32
BF16 = jnp.bfloat16

SSM_HEAD_DIM = 64
SSM_GROUPS = 4
SSM_STATE = 128
MLA_HEADS = 16
QK_NOPE = 64
QK_ROPE = 32
V_HEAD = 64
ROPE_BASE = 10000.0
EPS = 1e-6
LOG2_E = math.log2(math.e)

LANES = 128
BF16_SUBLANES = 16
HEAD_LANES = 128
SSD_CHUNK = 256
CONV_HALO = 8
VMEM_LIMIT = 56 * 1024 * 1024


def _params(*sem):
    return pltpu.CompilerParams(dimension_semantics=sem, vmem_limit_bytes=VMEM_LIMIT)


def _silu(x):
    return x * (1.0 / (1.0 + jnp.exp(-x)))


def _softplus(x):
    return jnp.maximum(x, 0.0) + jnp.log1p(jnp.exp(-jnp.abs(x)))


def _rms_scale(x):
    return x * lax.rsqrt(jnp.mean(x * x, axis=-1, keepdims=True) + EPS)


def _dot(a, b):
    return jnp.dot(a, b, preferred_element_type=F32)


def _split2(x):
    hi = x.astype(BF16)
    lo = (x - hi.astype(F32)).astype(BF16)
    return jnp.concatenate([hi, lo], axis=-1)


def _ssm_in_proj_kernel(x_ref, g_ref, w_ref, z_ref, xbc_ref, dt_ref):
    nz = z_ref.shape[1]
    nx = xbc_ref.shape[1]
    hn = (_rms_scale(x_ref[...]) * g_ref[...]).astype(BF16)
    z_ref[...] = _dot(hn, w_ref[:, :nz]).astype(BF16)
    xbc_ref[...] = _dot(hn, w_ref[:, nz:nz + nx]).astype(BF16)
    dt_ref[...] = _dot(hn, w_ref[:, nz + nx:])


def _ssm_in_proj(x2d, g, w, d_inner, conv_dim, tm):
    T, D = x2d.shape
    n_dt = w.shape[1] - d_inner - conv_dim
    full = lambda a: pl.BlockSpec(a.shape, lambda i: (0, 0))
    rows = lambda n: pl.BlockSpec((tm, n), lambda i: (i, 0))
    return pl.pallas_call(
        _ssm_in_proj_kernel,
        grid=(T // tm,),
        in_specs=[rows(D), full(g), full(w)],
        out_specs=[rows(d_inner), rows(conv_dim), rows(n_dt)],
        out_shape=[jax.ShapeDtypeStruct((T, d_inner), BF16),
                   jax.ShapeDtypeStruct((T, conv_dim), BF16),
                   jax.ShapeDtypeStruct((T, n_dt), F32)],
        compiler_params=_params("arbitrary"),
        name="ssm_in_proj",
    )(x2d, g, w)


def _ssd_kernel(z_ref, xbc_ref, dt_ref, x_ref, convw_ref, convb_ref, dtb_ref, alog_ref, dskip_ref,
                gout_ref, wout_ref, e2_ref, shift_ref, o_ref, hbuf, state, ybuf, *, d_inner, n_groups, d_state):
    Q = SSD_CHUNK
    gw = d_inner // n_groups
    heads_per_group = gw // SSM_HEAD_DIM
    pairs_per_group = gw // LANES
    kw = convw_ref.shape[0]

    @pl.when(pl.program_id(1) == 0)
    def _():
        hbuf[0:CONV_HALO, :] = jnp.zeros((CONV_HALO, hbuf.shape[1]), F32)
        state[...] = jnp.zeros(state.shape, F32)

    u_bf = xbc_ref[...]
    u = u_bf.astype(F32)
    shifted = _dot(shift_ref[...], u_bf)
    conv = convb_ref[...] + convw_ref[kw - 1:kw, :] * u
    for s in range(1, kw):
        conv = conv + convw_ref[kw - 1 - s:kw - s, :] * shifted[(s - 1) * Q:s * Q, :]
    hbuf[CONV_HALO:2 * CONV_HALO, :] = u[0:CONV_HALO, :]
    head = convb_ref[...]
    for k in range(kw):
        off = CONV_HALO - (kw - 1 - k)
        head = head + convw_ref[k:k + 1, :] * hbuf[off:off + CONV_HALO, :]
    hbuf[0:CONV_HALO, :] = u[Q - CONV_HALO:Q, :]
    conv = jnp.concatenate([head, conv[CONV_HALO:, :]], axis=0)
    xbc = _silu(conv)
    xs = xbc[:, :d_inner]

    dt = _softplus(dt_ref[...] + dtb_ref[...])
    dA = dt * (-jnp.exp(alog_ref[...]))
    row = lax.broadcasted_iota(jnp.int32, (Q, Q), 0)
    col = lax.broadcasted_iota(jnp.int32, (Q, Q), 1)
    causal = row >= col
    tril = causal.astype(BF16)
    hi = dA.astype(BF16)
    mid = (dA - hi.astype(F32)).astype(BF16)
    lo = (dA - hi.astype(F32) - mid.astype(F32)).astype(BF16)
    cum = (_dot(tril, hi) + _dot(tril, mid) + _dot(tril, lo)) * LOG2_E
    cum_t = cum.T
    cum_end = cum[Q - 1:Q, :]

    e2 = e2_ref[...]
    dt_x = _dot(_split2(dt), e2)
    ecum_x = _dot(_split2(jnp.exp2(cum)), e2)
    wend_x = _dot(_split2(jnp.exp2(cum_end - cum) * dt), e2)
    edec_x = _dot(_split2(jnp.broadcast_to(jnp.exp2(cum_end), (8, LANES))), e2)[0:1, :]

    lane = lax.broadcasted_iota(jnp.int32, (Q, LANES), 1)
    first_half = lane < SSM_HEAD_DIM
    xdt = xs * dt_x

    for g in range(n_groups):
        b_g = xbc[:, d_inner + g * d_state:d_inner + (g + 1) * d_state].astype(BF16)
        c_off = d_inner + n_groups * d_state
        c_g = xbc[:, c_off + g * d_state:c_off + (g + 1) * d_state].astype(BF16)
        cb = lax.dot_general(c_g, b_g, (((1,), (1,)), ((), ())), preferred_element_type=F32)
        for p in range(pairs_per_group):
            c0 = g * gw + p * LANES
            x_pair = xdt[:, c0:c0 + LANES]
            acc = None
            for hh in range(2):
                h = g * heads_per_group + 2 * p + hh
                seg = cum[:, h:h + 1] - cum_t[h:h + 1, :]
                m = jnp.where(causal, cb * jnp.exp2(seg), 0.0).astype(BF16)
                keep = first_half if hh == 0 else jnp.logical_not(first_half)
                x_h = jnp.where(keep, x_pair, 0.0).astype(BF16)
                part = _dot(m, x_h)
                acc = part if acc is None else acc + part
            ybuf[:, c0:c0 + LANES] = acc
        gs = slice(g * gw, (g + 1) * gw)
        st = state[g]
        y_inter = _dot(c_g, st.astype(BF16)) * ecum_x[:, gs]
        ybuf[:, gs] = ybuf[:, gs] + y_inter
        xw = (xs[:, gs] * wend_x[:, gs]).astype(BF16)
        upd = lax.dot_general(b_g, xw, (((0,), (0,)), ((), ())), preferred_element_type=F32)
        state[g] = st * edec_x[:, gs] + upd

    y = (ybuf[...] + dskip_ref[...] * xs) * _silu(z_ref[...].astype(F32))
    for g in range(n_groups):
        gs = slice(g * gw, (g + 1) * gw)
        ybuf[:, gs] = _rms_scale(y[:, gs])
    yn = (ybuf[...] * gout_ref[...]).astype(BF16)
    o_ref[...] = x_ref[...] + _dot(yn, wout_ref[...])


def _ssd_mixer(z, xbc, dt_raw, x2d, convw, convb, dtb, alog, dskip, gout, wout, e2, shift, batch):
    T, d_inner = z.shape
    D = x2d.shape[1]
    conv_dim = xbc.shape[1]
    n_chunks = T // batch // SSD_CHUNK
    gw = d_inner // SSM_GROUPS
    full = lambda a: pl.BlockSpec(a.shape, lambda b, c: (0, 0))
    rows = lambda n: pl.BlockSpec((SSD_CHUNK, n), lambda b, c: (b * n_chunks + c, 0))
    kernel = functools.partial(_ssd_kernel, d_inner=d_inner, n_groups=SSM_GROUPS, d_state=SSM_STATE)
    return pl.pallas_call(
        kernel,
        grid=(batch, n_chunks),
        in_specs=[rows(d_inner), rows(conv_dim), rows(LANES), rows(D), full(convw), full(convb), full(dtb),
                  full(alog), full(dskip), full(gout), full(wout), full(e2), full(shift)],
        out_specs=rows(D),
        out_shape=jax.ShapeDtypeStruct((T, D), F32),
        scratch_shapes=[pltpu.VMEM((2 * CONV_HALO, conv_dim), F32),
                        pltpu.VMEM((SSM_GROUPS, SSM_STATE, gw), F32),
                        pltpu.VMEM((SSD_CHUNK, d_inner), F32)],
        compiler_params=_params("arbitrary", "arbitrary"),
        name="ssd_mixer",
    )(z, xbc, dt_raw, x2d, convw, convb, dtb, alog, dskip, gout, wout, e2, shift)


def _nt_dot(a, b):
    return lax.dot_general(a, b, (((1,), (1,)), ((), ())), preferred_element_type=F32)


def _mla_proj_kernel(h_ref, pos_ref, invf_ref, gkv_ref, gq_ref, wdc_ref, wdr_ref,
                     gckv_ref, wuk_ref, wuvt_ref, wcq_ref, wgate_ref, gqn_ref, wuqt_ref,
                     qt_ref, k_ref, vt_ref, gate_ref, *, scale):
    tm = h_ref.shape[0]
    half = QK_ROPE // 2
    hs = _rms_scale(h_ref[...])
    hkv = (hs * gkv_ref[...]).astype(BF16)
    hq = (hs * gq_ref[...]).astype(BF16)

    ang = invf_ref[...] * pos_ref[...].astype(F32)
    cos = jnp.cos(ang)
    sin = jnp.sin(ang)

    ones = jnp.ones((QK_NOPE, tm), F32)
    tail = HEAD_LANES - QK_NOPE - QK_ROPE
    cos_k = jnp.concatenate([ones, cos, cos, ones[:tail]], axis=0).T
    sin_k = jnp.concatenate([0.0 * ones, sin, sin, 0.0 * ones[:tail]], axis=0).T
    ckv = (_rms_scale(_dot(hkv, wdc_ref[...])) * gckv_ref[...]).astype(BF16)
    kr2 = _dot(hkv, wdr_ref[...])
    k_rope = kr2[:, :HEAD_LANES] * cos_k + kr2[:, HEAD_LANES:] * sin_k
    vt_ref[0] = _nt_dot(wuvt_ref[...], ckv).astype(BF16)
    k_nope = _dot(ckv, wuk_ref[...])

    gate_ref[...] = _dot(hq, wgate_ref[...]).astype(BF16)
    cq = (_rms_scale(_dot(hq, wcq_ref[...])) * gqn_ref[...]).astype(BF16)
    qm = _nt_dot(wuqt_ref[...], cq)
    cos_q = cos * scale
    sin_q = sin * scale
    for h in range(k_ref.shape[1] // HEAD_LANES):
        r0 = h * HEAD_LANES
        r1 = r0 + QK_NOPE
        r2 = r1 + half
        r3 = r2 + half
        k_ref[:, r0:r0 + HEAD_LANES] = (k_nope[:, r0:r0 + HEAD_LANES] + k_rope).astype(BF16)
        x1 = qm[r1:r2, :]
        x2 = qm[r2:r3, :]
        qt_ref[0, r0:r1, :] = (qm[r0:r1, :] * scale).astype(BF16)
        qt_ref[0, r1:r2, :] = (x1 * cos_q - x2 * sin_q).astype(BF16)
        qt_ref[0, r2:r3, :] = (x1 * sin_q + x2 * cos_q).astype(BF16)
        qt_ref[0, r3:r0 + HEAD_LANES, :] = qm[r3:r0 + HEAD_LANES, :].astype(BF16)


def _mla_proj(h1, pos, invf, gkv, gq, wdc, wdr, gckv, wuk, wuvt, wcq, wgate, gqn, wuqt, tm, scale):
    T, D = h1.shape
    full = lambda a: pl.BlockSpec(a.shape, lambda i: (0, 0))
    rows = lambda n: pl.BlockSpec((tm, n), lambda i: (i, 0))
    cols = lambda n: pl.BlockSpec((1, n, tm), lambda i: (i, 0, 0))
    hq = wuqt.shape[0]
    hv = wuvt.shape[0]
    hg = wgate.shape[1]
    weights = (invf, gkv, gq, wdc, wdr, gckv, wuk, wuvt, wcq, wgate, gqn, wuqt)
    return pl.pallas_call(
        functools.partial(_mla_proj_kernel, scale=scale),
        grid=(T // tm,),
        in_specs=[rows(D), pl.BlockSpec((1, tm), lambda i: (0, i))] + [full(w) for w in weights],
        out_specs=[cols(hq), rows(hq), cols(hv), rows(hg)],
        out_shape=[jax.ShapeDtypeStruct((T // tm, hq, tm), BF16), jax.ShapeDtypeStruct((T, hq), BF16),
                   jax.ShapeDtypeStruct((T // tm, hv, tm), BF16), jax.ShapeDtypeStruct((T, hg), BF16)],
        compiler_params=_params("arbitrary"),
        name="mla_proj",
    )(h1, pos.reshape(1, T), *weights)


ATTN_Q_SPLIT = 2
ATTN_LAG_UNITS = 1


def _attn_kernel(qt_ref, k_ref, vt_ref, gate_ref, o_ref, *scratch, tq):
    n_units = 2 * ATTN_Q_SPLIT
    s_refs = scratch[:n_units]
    p_refs = scratch[n_units:2 * n_units]
    mx_ref, m_ref, alpha_ref, l_ref, acc_ref = scratch[2 * n_units:]
    qc = tq // ATTN_Q_SPLIT
    qi = pl.program_id(2)
    lead = range(n_units - ATTN_LAG_UNITS)
    lag = range(n_units - ATTN_LAG_UNITS, n_units)

    m_ref[...] = jnp.full(m_ref.shape, -jnp.inf, F32)
    l_ref[...] = jnp.zeros(l_ref.shape, F32)
    acc_ref[...] = jnp.zeros(acc_ref.shape, F32)
    for u in lag:
        p_refs[u][...] = jnp.zeros(p_refs[u].shape, BF16)
    alpha_ref[...] = jnp.ones(alpha_ref.shape, F32)

    def scores(u, j):
        h, c = divmod(u, ATTN_Q_SPLIT)
        hs = slice(h * HEAD_LANES, (h + 1) * HEAD_LANES)
        ks = pl.ds(pl.multiple_of(j * tq, tq), tq)
        s = _dot(k_ref[ks, hs], qt_ref[0, hs, c * qc:(c + 1) * qc])
        s_refs[u][...] = s
        mx_ref[u] = jnp.max(s, axis=0, keepdims=True)

    def diag_keys(u):
        return (u % ATTN_Q_SPLIT + 1) * qc

    def probs(u, masked):
        c = u % ATTN_Q_SPLIT
        nk = diag_keys(u) if masked else tq
        s = s_refs[u][0:nk, :]
        if masked:
            key = lax.broadcasted_iota(jnp.int32, (nk, qc), 0)
            qry = lax.broadcasted_iota(jnp.int32, (nk, qc), 1) + c * qc
            s = jnp.where(key <= qry, s, -jnp.inf)
            mx = jnp.max(s, axis=0, keepdims=True)
        else:
            mx = mx_ref[u]
        m_old = m_ref[u]
        m_new = jnp.maximum(m_old, mx)
        alpha_ref[u] = jnp.exp2(m_old - m_new)
        m_ref[u] = m_new
        p_refs[u][0:nk, :] = jnp.exp2(s - m_new).astype(BF16)

    def values(u, j, masked=False):
        h = u // ATTN_Q_SPLIT
        vs = slice(h * V_HEAD, (h + 1) * V_HEAD)
        nk = diag_keys(u) if masked else tq
        ones_rows = jnp.ones((BF16_SUBLANES, nk), BF16)
        pv = _dot(jnp.concatenate([vt_ref[j, vs, 0:nk], ones_rows], axis=0), p_refs[u][0:nk, :])
        alpha = alpha_ref[u]
        acc_ref[u] = alpha * acc_ref[u] + pv[:V_HEAD, :]
        l_ref[u] = alpha * l_ref[u] + pv[V_HEAD:V_HEAD + 1, :]

    for u in lead:
        scores(u, 0)

    def body(j, _):
        for u in lag:
            values(u, jnp.maximum(j - 1, 0))
        for u in lag:
            scores(u, j)
        for u in lead:
            probs(u, masked=False)
            values(u, j)
            scores(u, j + 1)
        for u in lag:
            probs(u, masked=False)
        return 0

    lax.fori_loop(0, qi, body, 0)
    for u in lag:
        values(u, jnp.maximum(qi - 1, 0))
    for u in lag:
        scores(u, qi)
    for u in range(n_units):
        probs(u, masked=True)
        values(u, qi, masked=True)

    o_t = jnp.concatenate(
        [jnp.concatenate([acc_ref[h * ATTN_Q_SPLIT + c] * (1.0 / l_ref[h * ATTN_Q_SPLIT + c])
                          for c in range(ATTN_Q_SPLIT)], axis=1) for h in range(2)], axis=0)
    o_ref[...] = (o_t.T * _silu(gate_ref[...].astype(F32))).astype(BF16)


def _mla_attn(qt, k, vt, gate, batch, tq):
    T = k.shape[0]
    S = T // batch
    n_q = S // tq
    n_pairs = vt.shape[1] // LANES
    n_units = 2 * ATTN_Q_SPLIT
    qc = tq // ATTN_Q_SPLIT
    stat = pltpu.VMEM((n_units, 1, qc), F32)
    return pl.pallas_call(
        functools.partial(_attn_kernel, tq=tq),
        grid=(batch, n_pairs, n_q),
        in_specs=[pl.BlockSpec((1, 2 * HEAD_LANES, tq), lambda b, p, i: (b * n_q + i, p, 0)),
                  pl.BlockSpec((S, 2 * HEAD_LANES), lambda b, p, i: (b, p)),
                  pl.BlockSpec((n_q, LANES, tq), lambda b, p, i: (b, p, 0)),
                  pl.BlockSpec((tq, LANES), lambda b, p, i: (b * n_q + i, p))],
        out_specs=pl.BlockSpec((tq, LANES), lambda b, p, i: (b * n_q + i, p)),
        out_shape=jax.ShapeDtypeStruct((T, vt.shape[1]), BF16),
        scratch_shapes=([pltpu.VMEM((tq, qc), F32)] * n_units + [pltpu.VMEM((tq, qc), BF16)] * n_units
                        + [stat, stat, stat, stat, pltpu.VMEM((n_units, V_HEAD, qc), F32)]),
        compiler_params=_params("arbitrary", "arbitrary", "arbitrary"),
        name="mla_attn",
    )(qt, k, vt, gate)


def _mla_out_kernel(o_ref, h_ref, w_ref, g_ref, out_ref):
    h = h_ref[...] + _dot(o_ref[...], w_ref[...])
    out_ref[...] = _rms_scale(h) * g_ref[...]


def _mla_out(o, h1, w, g, tm):
    T, D = h1.shape
    full = lambda a: pl.BlockSpec(a.shape, lambda i: (0, 0))
    rows = lambda n: pl.BlockSpec((tm, n), lambda i: (i, 0))
    return pl.pallas_call(
        _mla_out_kernel,
        grid=(T // tm,),
        in_specs=[rows(o.shape[1]), rows(D), full(w), full(g)],
        out_specs=rows(D),
        out_shape=jax.ShapeDtypeStruct((T, D), F32),
        compiler_params=_params("arbitrary"),
        name="mla_out",
    )(o, h1, w, g)


def _pad_cols(a, n):
    return jnp.pad(a, ((0, 0), (0, n - a.shape[1])))


def _head_slots(w_nope, w_rope):
    k, h = w_nope.shape[0], w_nope.shape[1]
    pad = jnp.zeros((k, h, HEAD_LANES - QK_NOPE - QK_ROPE), w_nope.dtype)
    return jnp.concatenate([w_nope, w_rope, pad], axis=-1).reshape(k, h * HEAD_LANES)


def _rotate_half_cols(w_rope):
    w1, w2 = jnp.split(w_rope, 2, axis=-1)
    return jnp.concatenate([-w2, w1], axis=-1)


def kernel(x, positions, g_pre, ssm_w_in, ssm_conv_w, ssm_conv_b, ssm_dt_bias, ssm_A_log, ssm_D, ssm_g_out,
           ssm_w_out, kv_g_in, kv_w_down, kv_g_latent, kv_w_up, mla_w_in, mla_g_q, mla_w_uq, mla_w_out, g_final):
    B, S, D = x.shape
    T = B * S
    d_inner = ssm_w_out.shape[1]
    n_heads = ssm_dt_bias.shape[1]
    conv_dim = ssm_conv_w.shape[2]
    kv_lora = kv_g_latent.shape[0]
    q_lora = mla_g_q.shape[1]
    H = MLA_HEADS
    assert d_inner == n_heads * SSM_HEAD_DIM and n_heads <= LANES
    assert S % SSD_CHUNK == 0 and ssm_conv_w.shape[1] - 1 <= CONV_HALO
    x2d = x.reshape(T, D)

    w_in = _pad_cols(ssm_w_in[0], d_inner + conv_dim + LANES).astype(BF16)
    z, xbc, dt_raw = _ssm_in_proj(x2d, g_pre[0][None, :], w_in, d_inner, conv_dim, tm=512)

    head_of_channel = jnp.arange(d_inner) // SSM_HEAD_DIM
    expand = (jnp.arange(LANES)[:, None] == head_of_channel[None, :]).astype(BF16)
    e2 = jnp.concatenate([expand, expand], axis=0)
    t_idx = jnp.arange(SSD_CHUNK)
    kw = ssm_conv_w.shape[1]
    shift = jnp.concatenate([(t_idx[:, None] - s == t_idx[None, :]) for s in range(1, kw)], axis=0).astype(BF16)
    h1 = _ssd_mixer(
        z, xbc, dt_raw, x2d, ssm_conv_w[0], ssm_conv_b[0][None, :],
        _pad_cols(ssm_dt_bias[0][None, :], LANES), _pad_cols(ssm_A_log[0][None, :], LANES),
        jnp.repeat(ssm_D[0], SSM_HEAD_DIM)[None, :], ssm_g_out[0][None, :], ssm_w_out[0].astype(BF16), e2, shift,
        batch=B)

    invf = (ROPE_BASE ** (-jnp.arange(0, QK_ROPE, 2, dtype=F32) / QK_ROPE))[:, None]
    wdc = kv_w_down[:, :kv_lora].astype(BF16)
    wr = kv_w_down[:, kv_lora:]
    slot1 = lambda w: jnp.pad(w, ((0, 0), (QK_NOPE, HEAD_LANES - QK_NOPE - QK_ROPE)))
    wdr = jnp.concatenate([slot1(wr), slot1(_rotate_half_cols(wr))], axis=1).astype(BF16)
    wup = kv_w_up.reshape(kv_lora, H, QK_NOPE + V_HEAD)
    wuk = _head_slots(wup[:, :, :QK_NOPE], jnp.zeros((kv_lora, H, QK_ROPE), F32)).astype(BF16)
    wuvt = wup[:, :, QK_NOPE:].reshape(kv_lora, H * V_HEAD).T.astype(BF16)
    w_in_b = mla_w_in[0]
    wcq = w_in_b[:, :q_lora].astype(BF16)
    wgate = w_in_b[:, q_lora:].astype(BF16)
    wq = mla_w_uq[0].reshape(q_lora, H, QK_NOPE + QK_ROPE)
    wq_nope, wq_rope = wq[:, :, :QK_NOPE], wq[:, :, QK_NOPE:]
    wuqt = _head_slots(wq_nope, wq_rope).T.astype(BF16)
    scale = float((QK_NOPE + QK_ROPE) ** -0.5 * math.log2(math.e))

    tq = 512
    qt, k, vt, gate = _mla_proj(
        h1, positions, invf, kv_g_in[None, :], g_pre[1][None, :], wdc, wdr, kv_g_latent[None, :],
        wuk, wuvt, wcq, wgate, mla_g_q[0][None, :], wuqt, tm=tq, scale=scale)
    o = _mla_attn(qt, k, vt, gate, batch=B, tq=tq)
    out = _mla_out(o, h1, mla_w_out[0].astype(BF16), g_final[None, :], tm=1024)
    return out.reshape(B, S, D)
```

```python
import functools
import math

import jax
import jax.numpy as jnp
from jax import lax
from jax.experimental import pallas as pl
from jax.experimental.pallas import tpu as pltpu

F32 = jnp.float32
BF16 = jnp.bfloat16

SSM_HEAD_DIM = 64
SSM_GROUPS = 4
SSM_STATE = 128
MLA_HEADS = 16
QK_NOPE = 64
QK_ROPE = 32
V_HEAD = 64
ROPE_BASE = 10000.0
EPS = 1e-6
LOG2_E = math.log2(math.e)

LANES = 128
BF16_SUBLANES = 16
HEAD_LANES = 128
SSD_CHUNK = 256
CONV_HALO = 8
VMEM_LIMIT = 56 * 1024 * 1024


def _params(*sem):
    return pltpu.CompilerParams(dimension_semantics=sem, vmem_limit_bytes=VMEM_LIMIT)


def _silu(x):
    return x * (1.0 / (1.0 + jnp.exp(-x)))


def _softplus(x):
    return jnp.maximum(x, 0.0) + jnp.log1p(jnp.exp(-jnp.abs(x)))


def _rms_scale(x):
    return x * lax.rsqrt(jnp.mean(x * x, axis=-1, keepdims=True) + EPS)


def _dot(a, b):
    return jnp.dot(a, b, preferred_element_type=F32)


def _split2(x):
    hi = x.astype(BF16)
    lo = (x - hi.astype(F32)).astype(BF16)
    return jnp.concatenate([hi, lo], axis=-1)


def _ssm_in_proj_kernel(x_ref, g_ref, w_ref, z_ref, xbc_ref, dt_ref):
    nz = z_ref.shape[1]
    nx = xbc_ref.shape[1]
    hn = (_rms_scale(x_ref[...]) * g_ref[...]).astype(BF16)
    z_ref[...] = _dot(hn, w_ref[:, :nz]).astype(BF16)
    xbc_ref[...] = _dot(hn, w_ref[:, nz:nz + nx]).astype(BF16)
    dt_ref[...] = _dot(hn, w_ref[:, nz + nx:])


def _ssm_in_proj(x2d, g, w, d_inner, conv_dim, tm):
    T, D = x2d.shape
    n_dt = w.shape[1] - d_inner - conv_dim
    full = lambda a: pl.BlockSpec(a.shape, lambda i: (0, 0))
    rows = lambda n: pl.BlockSpec((tm, n), lambda i: (i, 0))
    return pl.pallas_call(
        _ssm_in_proj_kernel,
        grid=(T // tm,),
        in_specs=[rows(D), full(g), full(w)],
        out_specs=[rows(d_inner), rows(conv_dim), rows(n_dt)],
        out_shape=[jax.ShapeDtypeStruct((T, d_inner), BF16),
                   jax.ShapeDtypeStruct((T, conv_dim), BF16),
                   jax.ShapeDtypeStruct((T, n_dt), F32)],
        compiler_params=_params("arbitrary"),
        name="ssm_in_proj",
    )(x2d, g, w)


def _ssd_kernel(z_ref, xbc_ref, dt_ref, x_ref, convw_ref, convb_ref, dtb_ref, alog_ref, dskip_ref,
                gout_ref, wout_ref, e2_ref, shift_ref, o_ref, hbuf, state, ybuf, *, d_inner, n_groups, d_state):
    Q = SSD_CHUNK
    gw = d_inner // n_groups
    heads_per_group = gw // SSM_HEAD_DIM
    pairs_per_group = gw // LANES
    kw = convw_ref.shape[0]

    @pl.when(pl.program_id(1) == 0)
    def _():
        hbuf[0:CONV_HALO, :] = jnp.zeros((CONV_HALO, hbuf.shape[1]), F32)
        state[...] = jnp.zeros(state.shape, F32)

    u_bf = xbc_ref[...]
    u = u_bf.astype(F32)
    shifted = _dot(shift_ref[...], u_bf)
    conv = convb_ref[...] + convw_ref[kw - 1:kw, :] * u
    for s in range(1, kw):
        conv = conv + convw_ref[kw - 1 - s:kw - s, :] * shifted[(s - 1) * Q:s * Q, :]
    hbuf[CONV_HALO:2 * CONV_HALO, :] = u[0:CONV_HALO, :]
    head = convb_ref[...]
    for k in range(kw):
        off = CONV_HALO - (kw - 1 - k)
        head = head + convw_ref[k:k + 1, :] * hbuf[off:off + CONV_HALO, :]
    hbuf[0:CONV_HALO, :] = u[Q - CONV_HALO:Q, :]
    conv = jnp.concatenate([head, conv[CONV_HALO:, :]], axis=0)
    xbc = _silu(conv)
    xs = xbc[:, :d_inner]

    dt = _softplus(dt_ref[...] + dtb_ref[...])
    dA = dt * (-jnp.exp(alog_ref[...]))
    row = lax.broadcasted_iota(jnp.int32, (Q, Q), 0)
    col = lax.broadcasted_iota(jnp.int32, (Q, Q), 1)
    causal = row >= col
    tril = causal.astype(BF16)
    hi = dA.astype(BF16)
    mid = (dA - hi.astype(F32)).astype(BF16)
    lo = (dA - hi.astype(F32) - mid.astype(F32)).astype(BF16)
    cum = (_dot(tril, hi) + _dot(tril, mid) + _dot(tril, lo)) * LOG2_E
    cum_t = cum.T
    cum_end = cum[Q - 1:Q, :]

    e2 = e2_ref[...]
    dt_x = _dot(_split2(dt), e2)
    ecum_x = _dot(_split2(jnp.exp2(cum)), e2)
    wend_x = _dot(_split2(jnp.exp2(cum_end - cum) * dt), e2)
    edec_x = _dot(_split2(jnp.broadcast_to(jnp.exp2(cum_end), (8, LANES))), e2)[0:1, :]

    lane = lax.broadcasted_iota(jnp.int32, (Q, LANES), 1)
    first_half = lane < SSM_HEAD_DIM
    xdt = xs * dt_x

    for g in range(n_groups):
        b_g = xbc[:, d_inner + g * d_state:d_inner + (g + 1) * d_state].astype(BF16)
        c_off = d_inner + n_groups * d_state
        c_g = xbc[:, c_off + g * d_state:c_off + (g + 1) * d_state].astype(BF16)
        cb = lax.dot_general(c_g, b_g, (((1,), (1,)), ((), ())), preferred_element_type=F32)
        for p in range(pairs_per_group):
            c0 = g * gw + p * LANES
            x_pair = xdt[:, c0:c0 + LANES]
            acc = None
            for hh in range(2):
                h = g * heads_per_group + 2 * p + hh
                seg = cum[:, h:h + 1] - cum_t[h:h + 1, :]
                m = jnp.where(causal, cb * jnp.exp2(seg), 0.0).astype(BF16)
                keep = first_half if hh == 0 else jnp.logical_not(first_half)
                x_h = jnp.where(keep, x_pair, 0.0).astype(BF16)
                part = _dot(m, x_h)
                acc = part if acc is None else acc + part
            ybuf[:, c0:c0 + LANES] = acc
        gs = slice(g * gw, (g + 1) * gw)
        st = state[g]
        y_inter = _dot(c_g, st.astype(BF16)) * ecum_x[:, gs]
        ybuf[:, gs] = ybuf[:, gs] + y_inter
        xw = (xs[:, gs] * wend_x[:, gs]).astype(BF16)
        upd = lax.dot_general(b_g, xw, (((0,), (0,)), ((), ())), preferred_element_type=F32)
        state[g] = st * edec_x[:, gs] + upd

    y = (ybuf[...] + dskip_ref[...] * xs) * _silu(z_ref[...].astype(F32))
    for g in range(n_groups):
        gs = slice(g * gw, (g + 1) * gw)
        ybuf[:, gs] = _rms_scale(y[:, gs])
    yn = (ybuf[...] * gout_ref[...]).astype(BF16)
    o_ref[...] = x_ref[...] + _dot(yn, wout_ref[...])


def _ssd_mixer(z, xbc, dt_raw, x2d, convw, convb, dtb, alog, dskip, gout, wout, e2, shift, batch):
    T, d_inner = z.shape
    D = x2d.shape[1]
    conv_dim = xbc.shape[1]
    n_chunks = T // batch // SSD_CHUNK
    gw = d_inner // SSM_GROUPS
    full = lambda a: pl.BlockSpec(a.shape, lambda b, c: (0, 0))
    rows = lambda n: pl.BlockSpec((SSD_CHUNK, n), lambda b, c: (b * n_chunks + c, 0))
    kernel = functools.partial(_ssd_kernel, d_inner=d_inner, n_groups=SSM_GROUPS, d_state=SSM_STATE)
    return pl.pallas_call(
        kernel,
        grid=(batch, n_chunks),
        in_specs=[rows(d_inner), rows(conv_dim), rows(LANES), rows(D), full(convw), full(convb), full(dtb),
                  full(alog), full(dskip), full(gout), full(wout), full(e2), full(shift)],
        out_specs=rows(D),
        out_shape=jax.ShapeDtypeStruct((T, D), F32),
        scratch_shapes=[pltpu.VMEM((2 * CONV_HALO, conv_dim), F32),
                        pltpu.VMEM((SSM_GROUPS, SSM_STATE, gw), F32),
                        pltpu.VMEM((SSD_CHUNK, d_inner), F32)],
        compiler_params=_params("arbitrary", "arbitrary"),
        name="ssd_mixer",
    )(z, xbc, dt_raw, x2d, convw, convb, dtb, alog, dskip, gout, wout, e2, shift)


def _nt_dot(a, b):
    return lax.dot_general(a, b, (((1,), (1,)), ((), ())), preferred_element_type=F32)


def _mla_proj_kernel(h_ref, pos_ref, invf_ref, gkv_ref, gq_ref, wdc_ref, wdr_ref,
                     gckv_ref, wuk_ref, wuvt_ref, wcq_ref, wgate_ref, gqn_ref, wuqt_ref,
                     qt_ref, k_ref, vt_ref, gate_ref, *, scale):
    tm = h_ref.shape[0]
    half = QK_ROPE // 2
    hs = _rms_scale(h_ref[...])
    hkv = (hs * gkv_ref[...]).astype(BF16)
    hq = (hs * gq_ref[...]).astype(BF16)

    ang = invf_ref[...] * pos_ref[...].astype(F32)
    cos = jnp.cos(ang)
    sin = jnp.sin(ang)

    ones = jnp.ones((QK_NOPE, tm), F32)
    tail = HEAD_LANES - QK_NOPE - QK_ROPE
    cos_k = jnp.concatenate([ones, cos, cos, ones[:tail]], axis=0).T
    sin_k = jnp.concatenate([0.0 * ones, sin, sin, 0.0 * ones[:tail]], axis=0).T
    ckv = (_rms_scale(_dot(hkv, wdc_ref[...])) * gckv_ref[...]).astype(BF16)
    kr2 = _dot(hkv, wdr_ref[...])
    k_rope = kr2[:, :HEAD_LANES] * cos_k + kr2[:, HEAD_LANES:] * sin_k
    vt_ref[0] = _nt_dot(wuvt_ref[...], ckv).astype(BF16)
    k_nope = _dot(ckv, wuk_ref[...])

    gate_ref[...] = _dot(hq, wgate_ref[...]).astype(BF16)
    cq = (_rms_scale(_dot(hq, wcq_ref[...])) * gqn_ref[...]).astype(BF16)
    qm = _nt_dot(wuqt_ref[...], cq)
    cos_q = cos * scale
    sin_q = sin * scale
    for h in range(k_ref.shape[1] // HEAD_LANES):
        r0 = h * HEAD_LANES
        r1 = r0 + QK_NOPE
        r2 = r1 + half
        r3 = r2 + half
        k_ref[:, r0:r0 + HEAD_LANES] = (k_nope[:, r0:r0 + HEAD_LANES] + k_rope).astype(BF16)
        x1 = qm[r1:r2, :]
        x2 = qm[r2:r3, :]
        qt_ref[0, r0:r1, :] = (qm[r0:r1, :] * scale).astype(BF16)
        qt_ref[0, r1:r2, :] = (x1 * cos_q - x2 * sin_q).astype(BF16)
        qt_ref[0, r2:r3, :] = (x1 * sin_q + x2 * cos_q).astype(BF16)
        qt_ref[0, r3:r0 + HEAD_LANES, :] = qm[r3:r0 + HEAD_LANES, :].astype(BF16)


def _mla_proj(h1, pos, invf, gkv, gq, wdc, wdr, gckv, wuk, wuvt, wcq, wgate, gqn, wuqt, tm, scale):
    T, D = h1.shape
    full = lambda a: pl.BlockSpec(a.shape, lambda i: (0, 0))
    rows = lambda n: pl.BlockSpec((tm, n), lambda i: (i, 0))
    cols = lambda n: pl.BlockSpec((1, n, tm), lambda i: (i, 0, 0))
    hq = wuqt.shape[0]
    hv = wuvt.shape[0]
    hg = wgate.shape[1]
    weights = (invf, gkv, gq, wdc, wdr, gckv, wuk, wuvt, wcq, wgate, gqn, wuqt)
    return pl.pallas_call(
        functools.partial(_mla_proj_kernel, scale=scale),
        grid=(T // tm,),
        in_specs=[rows(D), pl.BlockSpec((1, tm), lambda i: (0, i))] + [full(w) for w in weights],
        out_specs=[cols(hq), rows(hq), cols(hv), rows(hg)],
        out_shape=[jax.ShapeDtypeStruct((T // tm, hq, tm), BF16), jax.ShapeDtypeStruct((T, hq), BF16),
                   jax.ShapeDtypeStruct((T // tm, hv, tm), BF16), jax.ShapeDtypeStruct((T, hg), BF16)],
        compiler_params=_params("arbitrary"),
        name="mla_proj",
    )(h1, pos.reshape(1, T), *weights)


ATTN_TILE = 1024
ATTN_UNIT_COLS = 256
ATTN_LAG_UNITS = 1


def _attn_kernel(qt_ref, k_ref, vt_ref, gate_ref, o_ref, *scratch, tm):
    tq = ATTN_TILE
    qc = ATTN_UNIT_COLS
    n_slices = tq // qc
    n_units = 2 * n_slices
    sub = tq // tm
    s_refs = scratch[:n_units]
    p_refs = scratch[n_units:2 * n_units]
    mx_ref, m_ref, alpha_ref, l_ref, acc_ref = scratch[2 * n_units:]
    qi = pl.program_id(2)
    lead = range(n_units - ATTN_LAG_UNITS)
    lag = range(n_units - ATTN_LAG_UNITS, n_units)

    m_ref[...] = jnp.full(m_ref.shape, -jnp.inf, F32)
    l_ref[...] = jnp.zeros(l_ref.shape, F32)
    acc_ref[...] = jnp.zeros(acc_ref.shape, F32)
    for u in lag:
        p_refs[u][...] = jnp.zeros(p_refs[u].shape, BF16)
    alpha_ref[...] = jnp.ones(alpha_ref.shape, F32)

    def scores(u, j):
        h, c = divmod(u, n_slices)
        hs = slice(h * HEAD_LANES, (h + 1) * HEAD_LANES)
        ks = pl.ds(pl.multiple_of(j * tq, tq), tq)
        t, off = divmod(c * qc, tm)
        s = _dot(k_ref[ks, hs], qt_ref[t, hs, off:off + qc])
        s_refs[u][...] = s
        mx_ref[u] = jnp.max(s, axis=0, keepdims=True)

    def diag_keys(u):
        return (u % n_slices + 1) * qc

    def probs(u, masked):
        c = u % n_slices
        nk = diag_keys(u) if masked else tq
        s = s_refs[u][0:nk, :]
        if masked:
            key = lax.broadcasted_iota(jnp.int32, (nk, qc), 0)
            qry = lax.broadcasted_iota(jnp.int32, (nk, qc), 1) + c * qc
            s = jnp.where(key <= qry, s, -jnp.inf)
            mx = jnp.max(s, axis=0, keepdims=True)
        else:
            mx = mx_ref[u]
        m_old = m_ref[u]
        m_new = jnp.maximum(m_old, mx)
        alpha_ref[u] = jnp.exp2(m_old - m_new)
        m_ref[u] = m_new
        p_refs[u][0:nk, :] = jnp.exp2(s - m_new).astype(BF16)

    def values(u, j, masked=False):
        h = u // n_slices
        vs = slice(h * V_HEAD, (h + 1) * V_HEAD)
        nk = diag_keys(u) if masked else tq
        pv = None
        for t in range(sub):
            n = min(nk - t * tm, tm)
            if n <= 0:
                break
            lhs = jnp.concatenate([vt_ref[j * sub + t, vs, 0:n], jnp.ones((BF16_SUBLANES, n), BF16)], axis=0)
            part = _dot(lhs, p_refs[u][t * tm:t * tm + n, :])
            pv = part if pv is None else pv + part
        alpha = alpha_ref[u]
        acc_ref[u] = alpha * acc_ref[u] + pv[:V_HEAD, :]
        l_ref[u] = alpha * l_ref[u] + pv[V_HEAD:V_HEAD + 1, :]

    for u in lead:
        scores(u, 0)

    def body(j, _):
        for u in lag:
            values(u, jnp.maximum(j - 1, 0))
        for u in lag:
            scores(u, j)
        for u in lead:
            probs(u, masked=False)
            values(u, j)
            scores(u, j + 1)
        for u in lag:
            probs(u, masked=False)
        return 0

    lax.fori_loop(0, qi, body, 0)
    for u in lag:
        values(u, jnp.maximum(qi - 1, 0))
    for u in lag:
        scores(u, qi)
    for u in range(n_units):
        probs(u, masked=True)
        values(u, qi, masked=True)

    o_t = jnp.concatenate(
        [jnp.concatenate([acc_ref[h * n_slices + c] * (1.0 / l_ref[h * n_slices + c])
                          for c in range(n_slices)], axis=1) for h in range(2)], axis=0)
    o_ref[...] = (o_t.T * _silu(gate_ref[...].astype(F32))).astype(BF16)


def _mla_attn(qt, k, vt, gate, batch):
    T = k.shape[0]
    tm = qt.shape[2]
    tq = ATTN_TILE
    qc = ATTN_UNIT_COLS
    S = T // batch
    n_q = S // tq
    sub = tq // tm
    n_pairs = vt.shape[1] // LANES
    n_units = 2 * (tq // qc)
    stat = pltpu.VMEM((n_units, 1, qc), F32)
    return pl.pallas_call(
        functools.partial(_attn_kernel, tm=tm),
        grid=(batch, n_pairs, n_q),
        in_specs=[pl.BlockSpec((sub, 2 * HEAD_LANES, tm), lambda b, p, i: (b * n_q + i, p, 0)),
                  pl.BlockSpec((S, 2 * HEAD_LANES), lambda b, p, i: (b, p)),
                  pl.BlockSpec((S // tm, LANES, tm), lambda b, p, i: (b, p, 0)),
                  pl.BlockSpec((tq, LANES), lambda b, p, i: (b * n_q + i, p))],
        out_specs=pl.BlockSpec((tq, LANES), lambda b, p, i: (b * n_q + i, p)),
        out_shape=jax.ShapeDtypeStruct((T, vt.shape[1]), BF16),
        scratch_shapes=([pltpu.VMEM((tq, qc), F32)] * n_units + [pltpu.VMEM((tq, qc), BF16)] * n_units
                        + [stat, stat, stat, stat, pltpu.VMEM((n_units, V_HEAD, qc), F32)]),
        compiler_params=_params("arbitrary", "arbitrary", "arbitrary"),
        name="mla_attn",
    )(qt, k, vt, gate)


def _mla_out_kernel(o_ref, h_ref, w_ref, g_ref, out_ref):
    h = h_ref[...] + _dot(o_ref[...], w_ref[...])
    out_ref[...] = _rms_scale(h) * g_ref[...]


def _mla_out(o, h1, w, g, tm):
    T, D = h1.shape
    full = lambda a: pl.BlockSpec(a.shape, lambda i: (0, 0))
    rows = lambda n: pl.BlockSpec((tm, n), lambda i: (i, 0))
    return pl.pallas_call(
        _mla_out_kernel,
        grid=(T // tm,),
        in_specs=[rows(o.shape[1]), rows(D), full(w), full(g)],
        out_specs=rows(D),
        out_shape=jax.ShapeDtypeStruct((T, D), F32),
        compiler_params=_params("arbitrary"),
        name="mla_out",
    )(o, h1, w, g)


def _pad_cols(a, n):
    return jnp.pad(a, ((0, 0), (0, n - a.shape[1])))


def _head_slots(w_nope, w_rope):
    k, h = w_nope.shape[0], w_nope.shape[1]
    pad = jnp.zeros((k, h, HEAD_LANES - QK_NOPE - QK_ROPE), w_nope.dtype)
    return jnp.concatenate([w_nope, w_rope, pad], axis=-1).reshape(k, h * HEAD_LANES)


def _rotate_half_cols(w_rope):
    w1, w2 = jnp.split(w_rope, 2, axis=-1)
    return jnp.concatenate([-w2, w1], axis=-1)


def kernel(x, positions, g_pre, ssm_w_in, ssm_conv_w, ssm_conv_b, ssm_dt_bias, ssm_A_log, ssm_D, ssm_g_out,
           ssm_w_out, kv_g_in, kv_w_down, kv_g_latent, kv_w_up, mla_w_in, mla_g_q, mla_w_uq, mla_w_out, g_final):
    B, S, D = x.shape
    T = B * S
    d_inner = ssm_w_out.shape[1]
    n_heads = ssm_dt_bias.shape[1]
    conv_dim = ssm_conv_w.shape[2]
    kv_lora = kv_g_latent.shape[0]
    q_lora = mla_g_q.shape[1]
    H = MLA_HEADS
    assert d_inner == n_heads * SSM_HEAD_DIM and n_heads <= LANES
    assert S % SSD_CHUNK == 0 and ssm_conv_w.shape[1] - 1 <= CONV_HALO
    x2d = x.reshape(T, D)

    w_in = _pad_cols(ssm_w_in[0], d_inner + conv_dim + LANES).astype(BF16)
    z, xbc, dt_raw = _ssm_in_proj(x2d, g_pre[0][None, :], w_in, d_inner, conv_dim, tm=512)

    head_of_channel = jnp.arange(d_inner) // SSM_HEAD_DIM
    expand = (jnp.arange(LANES)[:, None] == head_of_channel[None, :]).astype(BF16)
    e2 = jnp.concatenate([expand, expand], axis=0)
    t_idx = jnp.arange(SSD_CHUNK)
    kw = ssm_conv_w.shape[1]
    shift = jnp.concatenate([(t_idx[:, None] - s == t_idx[None, :]) for s in range(1, kw)], axis=0).astype(BF16)
    h1 = _ssd_mixer(
        z, xbc, dt_raw, x2d, ssm_conv_w[0], ssm_conv_b[0][None, :],
        _pad_cols(ssm_dt_bias[0][None, :], LANES), _pad_cols(ssm_A_log[0][None, :], LANES),
        jnp.repeat(ssm_D[0], SSM_HEAD_DIM)[None, :], ssm_g_out[0][None, :], ssm_w_out[0].astype(BF16), e2, shift,
        batch=B)

    invf = (ROPE_BASE ** (-jnp.arange(0, QK_ROPE, 2, dtype=F32) / QK_ROPE))[:, None]
    wdc = kv_w_down[:, :kv_lora].astype(BF16)
    wr = kv_w_down[:, kv_lora:]
    slot1 = lambda w: jnp.pad(w, ((0, 0), (QK_NOPE, HEAD_LANES - QK_NOPE - QK_ROPE)))
    wdr = jnp.concatenate([slot1(wr), slot1(_rotate_half_cols(wr))], axis=1).astype(BF16)
    wup = kv_w_up.reshape(kv_lora, H, QK_NOPE + V_HEAD)
    wuk = _head_slots(wup[:, :, :QK_NOPE], jnp.zeros((kv_lora, H, QK_ROPE), F32)).astype(BF16)
    wuvt = wup[:, :, QK_NOPE:].reshape(kv_lora, H * V_HEAD).T.astype(BF16)
    w_in_b = mla_w_in[0]
    wcq = w_in_b[:, :q_lora].astype(BF16)
    wgate = w_in_b[:, q_lora:].astype(BF16)
    wq = mla_w_uq[0].reshape(q_lora, H, QK_NOPE + QK_ROPE)
    wq_nope, wq_rope = wq[:, :, :QK_NOPE], wq[:, :, QK_NOPE:]
    wuqt = _head_slots(wq_nope, wq_rope).T.astype(BF16)
    scale = float((QK_NOPE + QK_ROPE) ** -0.5 * math.log2(math.e))

    qt, k, vt, gate = _mla_proj(
        h1, positions, invf, kv_g_in[None, :], g_pre[1][None, :], wdc, wdr, kv_g_latent[None, :],
        wuk, wuvt, wcq, wgate, mla_g_q[0][None, :], wuqt, tm=512, scale=scale)
    o = _mla_attn(qt, k, vt, gate, batch=B)
    out = _mla_out(o, h1, mla_w_out[0].astype(BF16), g_final[None, :], tm=1024)
    return out.reshape(B, S, D)
```

```python
import functools
import math

import jax
import jax.numpy as jnp
from jax import lax
from jax.experimental import pallas as pl
from jax.experimental.pallas import tpu as pltpu

F32 = jnp.float32
BF16 = jnp.bfloat16

SSM_HEAD_DIM = 64
SSM_GROUPS = 4
SSM_STATE = 128
MLA_HEADS = 16
QK_NOPE = 64
QK_ROPE = 32
V_HEAD = 64
ROPE_BASE = 10000.0
EPS = 1e-6
LOG2_E = math.log2(math.e)

LANES = 128
BF16_SUBLANES = 16
HEAD_LANES = 128
SSD_CHUNK = 256
CONV_HALO = 8
VMEM_LIMIT = 56 * 1024 * 1024


def _params(*sem):
    return pltpu.CompilerParams(dimension_semantics=sem, vmem_limit_bytes=VMEM_LIMIT)


def _silu(x):
    return x * (1.0 / (1.0 + jnp.exp(-x)))


def _softplus(x):
    return jnp.maximum(x, 0.0) + jnp.log1p(jnp.exp(-jnp.abs(x)))


def _rms_scale(x):
    return x * lax.rsqrt(jnp.mean(x * x, axis=-1, keepdims=True) + EPS)


def _dot(a, b):
    return jnp.dot(a, b, preferred_element_type=F32)


def _split2(x):
    hi = x.astype(BF16)
    lo = (x - hi.astype(F32)).astype(BF16)
    return jnp.concatenate([hi, lo], axis=-1)


def _ssm_in_proj_kernel(x_ref, g_ref, w_ref, z_ref, xbc_ref, dt_ref):
    nz = z_ref.shape[1]
    nx = xbc_ref.shape[1]
    hn = (_rms_scale(x_ref[...]) * g_ref[...]).astype(BF16)
    z_ref[...] = _dot(hn, w_ref[:, :nz]).astype(BF16)
    xbc_ref[...] = _dot(hn, w_ref[:, nz:nz + nx]).astype(BF16)
    dt_ref[...] = _dot(hn, w_ref[:, nz + nx:])


def _ssm_in_proj(x2d, g, w, d_inner, conv_dim, tm):
    T, D = x2d.shape
    n_dt = w.shape[1] - d_inner - conv_dim
    full = lambda a: pl.BlockSpec(a.shape, lambda i: (0, 0))
    rows = lambda n: pl.BlockSpec((tm, n), lambda i: (i, 0))
    return pl.pallas_call(
        _ssm_in_proj_kernel,
        grid=(T // tm,),
        in_specs=[rows(D), full(g), full(w)],
        out_specs=[rows(d_inner), rows(conv_dim), rows(n_dt)],
        out_shape=[jax.ShapeDtypeStruct((T, d_inner), BF16),
                   jax.ShapeDtypeStruct((T, conv_dim), BF16),
                   jax.ShapeDtypeStruct((T, n_dt), F32)],
        compiler_params=_params("arbitrary"),
        name="ssm_in_proj",
    )(x2d, g, w)


def _ssd_kernel(z_ref, xbc_ref, dt_ref, x_ref, convw_ref, convb_ref, dtb_ref, alog_ref, dskip_ref,
                gout_ref, wout_ref, e2_ref, shift_ref, o_ref, hbuf, state, ybuf, *, d_inner, n_groups, d_state):
    Q = SSD_CHUNK
    gw = d_inner // n_groups
    heads_per_group = gw // SSM_HEAD_DIM
    pairs_per_group = gw // LANES
    kw = convw_ref.shape[0]

    @pl.when(pl.program_id(1) == 0)
    def _():
        hbuf[0:CONV_HALO, :] = jnp.zeros((CONV_HALO, hbuf.shape[1]), F32)
        state[...] = jnp.zeros(state.shape, F32)

    u_bf = xbc_ref[...]
    u = u_bf.astype(F32)
    shifted = _dot(shift_ref[...], u_bf)
    conv = convb_ref[...] + convw_ref[kw - 1:kw, :] * u
    for s in range(1, kw):
        conv = conv + convw_ref[kw - 1 - s:kw - s, :] * shifted[(s - 1) * Q:s * Q, :]
    hbuf[CONV_HALO:2 * CONV_HALO, :] = u[0:CONV_HALO, :]
    head = convb_ref[...]
    for k in range(kw):
        off = CONV_HALO - (kw - 1 - k)
        head = head + convw_ref[k:k + 1, :] * hbuf[off:off + CONV_HALO, :]
    hbuf[0:CONV_HALO, :] = u[Q - CONV_HALO:Q, :]
    conv = jnp.concatenate([head, conv[CONV_HALO:, :]], axis=0)
    xbc = _silu(conv)
    xs = xbc[:, :d_inner]

    dt = _softplus(dt_ref[...] + dtb_ref[...])
    dA = dt * (-jnp.exp(alog_ref[...]))
    row = lax.broadcasted_iota(jnp.int32, (Q, Q), 0)
    col = lax.broadcasted_iota(jnp.int32, (Q, Q), 1)
    causal = row >= col
    tril = causal.astype(BF16)
    hi = dA.astype(BF16)
    mid = (dA - hi.astype(F32)).astype(BF16)
    lo = (dA - hi.astype(F32) - mid.astype(F32)).astype(BF16)
    cum = (_dot(tril, hi) + _dot(tril, mid) + _dot(tril, lo)) * LOG2_E
    cum_t = cum.T
    cum_end = cum[Q - 1:Q, :]

    e2 = e2_ref[...]
    dt_x = _dot(_split2(dt), e2)
    ecum_x = _dot(_split2(jnp.exp2(cum)), e2)
    wend_x = _dot(_split2(jnp.exp2(cum_end - cum) * dt), e2)
    edec_x = _dot(_split2(jnp.broadcast_to(jnp.exp2(cum_end), (8, LANES))), e2)[0:1, :]

    lane = lax.broadcasted_iota(jnp.int32, (Q, LANES), 1)
    first_half = lane < SSM_HEAD_DIM
    xdt = xs * dt_x

    for g in range(n_groups):
        b_g = xbc[:, d_inner + g * d_state:d_inner + (g + 1) * d_state].astype(BF16)
        c_off = d_inner + n_groups * d_state
        c_g = xbc[:, c_off + g * d_state:c_off + (g + 1) * d_state].astype(BF16)
        cb = lax.dot_general(c_g, b_g, (((1,), (1,)), ((), ())), preferred_element_type=F32)
        gs = slice(g * gw, (g + 1) * gw)
        st = state[g]
        y_inter = _dot(c_g, st.astype(BF16)) * ecum_x[:, gs]
        xw = (xs[:, gs] * wend_x[:, gs]).astype(BF16)
        upd = lax.dot_general(b_g, xw, (((0,), (0,)), ((), ())), preferred_element_type=F32)
        state[g] = st * edec_x[:, gs] + upd
        for p in range(pairs_per_group):
            c0 = g * gw + p * LANES
            x_pair = xdt[:, c0:c0 + LANES]
            acc = y_inter[:, p * LANES:(p + 1) * LANES]
            for hh in range(2):
                h = g * heads_per_group + 2 * p + hh
                seg = cum[:, h:h + 1] - cum_t[h:h + 1, :]
                m = jnp.where(causal, cb * jnp.exp2(seg), 0.0).astype(BF16)
                keep = first_half if hh == 0 else jnp.logical_not(first_half)
                x_h = jnp.where(keep, x_pair, 0.0).astype(BF16)
                acc = acc + _dot(m, x_h)
            ybuf[:, c0:c0 + LANES] = acc

    y = (ybuf[...] + dskip_ref[...] * xs) * _silu(z_ref[...].astype(F32))
    for g in range(n_groups):
        gs = slice(g * gw, (g + 1) * gw)
        ybuf[:, gs] = _rms_scale(y[:, gs])
    yn = (ybuf[...] * gout_ref[...]).astype(BF16)
    o_ref[...] = x_ref[...] + _dot(yn, wout_ref[...])


def _ssd_mixer(z, xbc, dt_raw, x2d, convw, convb, dtb, alog, dskip, gout, wout, e2, shift, batch):
    T, d_inner = z.shape
    D = x2d.shape[1]
    conv_dim = xbc.shape[1]
    n_chunks = T // batch // SSD_CHUNK
    gw = d_inner // SSM_GROUPS
    full = lambda a: pl.BlockSpec(a.shape, lambda b, c: (0, 0))
    rows = lambda n: pl.BlockSpec((SSD_CHUNK, n), lambda b, c: (b * n_chunks + c, 0))
    kernel = functools.partial(_ssd_kernel, d_inner=d_inner, n_groups=SSM_GROUPS, d_state=SSM_STATE)
    return pl.pallas_call(
        kernel,
        grid=(batch, n_chunks),
        in_specs=[rows(d_inner), rows(conv_dim), rows(LANES), rows(D), full(convw), full(convb), full(dtb),
                  full(alog), full(dskip), full(gout), full(wout), full(e2), full(shift)],
        out_specs=rows(D),
        out_shape=jax.ShapeDtypeStruct((T, D), F32),
        scratch_shapes=[pltpu.VMEM((2 * CONV_HALO, conv_dim), F32),
                        pltpu.VMEM((SSM_GROUPS, SSM_STATE, gw), F32),
                        pltpu.VMEM((SSD_CHUNK, d_inner), F32)],
        compiler_params=_params("arbitrary", "arbitrary"),
        name="ssd_mixer",
    )(z, xbc, dt_raw, x2d, convw, convb, dtb, alog, dskip, gout, wout, e2, shift)


def _nt_dot(a, b):
    return lax.dot_general(a, b, (((1,), (1,)), ((), ())), preferred_element_type=F32)


def _mla_proj_kernel(h_ref, pos_ref, invf_ref, gkv_ref, gq_ref, wdc_ref, wdr_ref,
                     gckv_ref, wuk_ref, wuvt_ref, wcq_ref, wgate_ref, gqn_ref, wuqt_ref,
                     qt_ref, k_ref, vt_ref, gate_ref, *, scale):
    tm = h_ref.shape[0]
    half = QK_ROPE // 2
    hs = _rms_scale(h_ref[...])
    hkv = (hs * gkv_ref[...]).astype(BF16)
    hq = (hs * gq_ref[...]).astype(BF16)

    ang = invf_ref[...] * pos_ref[...].astype(F32)
    cos = jnp.cos(ang)
    sin = jnp.sin(ang)

    ones = jnp.ones((QK_NOPE, tm), F32)
    tail = HEAD_LANES - QK_NOPE - QK_ROPE
    cos_k = jnp.concatenate([ones, cos, cos, ones[:tail]], axis=0).T
    sin_k = jnp.concatenate([0.0 * ones, sin, sin, 0.0 * ones[:tail]], axis=0).T
    ckv = (_rms_scale(_dot(hkv, wdc_ref[...])) * gckv_ref[...]).astype(BF16)
    kr2 = _dot(hkv, wdr_ref[...])
    k_rope = kr2[:, :HEAD_LANES] * cos_k + kr2[:, HEAD_LANES:] * sin_k
    vt_ref[0] = _nt_dot(wuvt_ref[...], ckv).astype(BF16)
    k_nope = _dot(ckv, wuk_ref[...])

    gate_ref[...] = _dot(hq, wgate_ref[...]).astype(BF16)
    cq = (_rms_scale(_dot(hq, wcq_ref[...])) * gqn_ref[...]).astype(BF16)
    qm = _nt_dot(wuqt_ref[...], cq)
    cos_q = cos * scale
    sin_q = sin * scale
    for h in range(k_ref.shape[1] // HEAD_LANES):
        r0 = h * HEAD_LANES
        r1 = r0 + QK_NOPE
        r2 = r1 + half
        r3 = r2 + half
        k_ref[:, r0:r0 + HEAD_LANES] = (k_nope[:, r0:r0 + HEAD_LANES] + k_rope).astype(BF16)
        x1 = qm[r1:r2, :]
        x2 = qm[r2:r3, :]
        qt_ref[0, r0:r1, :] = (qm[r0:r1, :] * scale).astype(BF16)
        qt_ref[0, r1:r2, :] = (x1 * cos_q - x2 * sin_q).astype(BF16)
        qt_ref[0, r2:r3, :] = (x1 * sin_q + x2 * cos_q).astype(BF16)
        qt_ref[0, r3:r0 + HEAD_LANES, :] = qm[r3:r0 + HEAD_LANES, :].astype(BF16)


def _mla_proj(h1, pos, invf, gkv, gq, wdc, wdr, gckv, wuk, wuvt, wcq, wgate, gqn, wuqt, tm, scale):
    T, D = h1.shape
    full = lambda a: pl.BlockSpec(a.shape, lambda i: (0, 0))
    rows = lambda n: pl.BlockSpec((tm, n), lambda i: (i, 0))
    cols = lambda n: pl.BlockSpec((1, n, tm), lambda i: (i, 0, 0))
    hq = wuqt.shape[0]
    hv = wuvt.shape[0]
    hg = wgate.shape[1]
    weights = (invf, gkv, gq, wdc, wdr, gckv, wuk, wuvt, wcq, wgate, gqn, wuqt)
    return pl.pallas_call(
        functools.partial(_mla_proj_kernel, scale=scale),
        grid=(T // tm,),
        in_specs=[rows(D), pl.BlockSpec((1, tm), lambda i: (0, i))] + [full(w) for w in weights],
        out_specs=[cols(hq), rows(hq), cols(hv), rows(hg)],
        out_shape=[jax.ShapeDtypeStruct((T // tm, hq, tm), BF16), jax.ShapeDtypeStruct((T, hq), BF16),
                   jax.ShapeDtypeStruct((T // tm, hv, tm), BF16), jax.ShapeDtypeStruct((T, hg), BF16)],
        compiler_params=_params("arbitrary"),
        name="mla_proj",
    )(h1, pos.reshape(1, T), *weights)


ATTN_TILE = 1024
ATTN_UNIT_COLS = 256
ATTN_LAG_UNITS = 1


def _attn_kernel(qt_ref, k_ref, vt_ref, gate_ref, o_ref, *scratch, tm):
    tq = ATTN_TILE
    qc = ATTN_UNIT_COLS
    n_slices = tq // qc
    n_units = 2 * n_slices
    sub = tq // tm
    s_refs = scratch[:n_units]
    p_refs = scratch[n_units:2 * n_units]
    mx_ref, m_ref, alpha_ref, l_ref, acc_ref = scratch[2 * n_units:]
    qi = pl.program_id(2)
    lead = range(n_units - ATTN_LAG_UNITS)
    lag = range(n_units - ATTN_LAG_UNITS, n_units)

    m_ref[...] = jnp.full(m_ref.shape, -jnp.inf, F32)
    l_ref[...] = jnp.zeros(l_ref.shape, F32)
    acc_ref[...] = jnp.zeros(acc_ref.shape, F32)
    for u in lag:
        p_refs[u][...] = jnp.zeros(p_refs[u].shape, BF16)
    alpha_ref[...] = jnp.ones(alpha_ref.shape, F32)

    def head_slice(u):
        h, r = divmod(u, n_slices)
        return h, n_slices - 1 - r

    def scores(u, j):
        h, c = head_slice(u)
        hs = slice(h * HEAD_LANES, (h + 1) * HEAD_LANES)
        ks = pl.ds(pl.multiple_of(j * tq, tq), tq)
        t, off = divmod(c * qc, tm)
        s = _dot(k_ref[ks, hs], qt_ref[t, hs, off:off + qc])
        s_refs[u][...] = s
        mx_ref[u] = jnp.max(s, axis=0, keepdims=True)

    def diag_keys(u):
        return (head_slice(u)[1] + 1) * qc

    def probs(u, masked):
        c = head_slice(u)[1]
        nk = diag_keys(u) if masked else tq
        s = s_refs[u][0:nk, :]
        if masked:
            key = lax.broadcasted_iota(jnp.int32, (nk, qc), 0)
            qry = lax.broadcasted_iota(jnp.int32, (nk, qc), 1) + c * qc
            s = jnp.where(key <= qry, s, -jnp.inf)
            mx = jnp.max(s, axis=0, keepdims=True)
        else:
            mx = mx_ref[u]
        m_old = m_ref[u]
        m_new = jnp.maximum(m_old, mx)
        alpha_ref[u] = jnp.exp2(m_old - m_new)
        m_ref[u] = m_new
        p_refs[u][0:nk, :] = jnp.exp2(s - m_new).astype(BF16)

    def values(u, j, masked=False):
        h = u // n_slices
        vs = slice(h * V_HEAD, (h + 1) * V_HEAD)
        nk = diag_keys(u) if masked else tq
        pv = None
        for t in range(sub):
            n = min(nk - t * tm, tm)
            if n <= 0:
                break
            lhs = jnp.concatenate([vt_ref[j * sub + t, vs, 0:n], jnp.ones((BF16_SUBLANES, n), BF16)], axis=0)
            part = _dot(lhs, p_refs[u][t * tm:t * tm + n, :])
            pv = part if pv is None else pv + part
        alpha = alpha_ref[u]
        acc_ref[u] = alpha * acc_ref[u] + pv[:V_HEAD, :]
        l_ref[u] = alpha * l_ref[u] + pv[V_HEAD:V_HEAD + 1, :]

    for u in lead:
        scores(u, 0)

    def body(j, _):
        for u in lag:
            values(u, jnp.maximum(j - 1, 0))
        for u in lag:
            scores(u, j)
        for u in lead:
            probs(u, masked=False)
            values(u, j)
            scores(u, j + 1)
        for u in lag:
            probs(u, masked=False)
        return 0

    lax.fori_loop(0, qi, body, 0)
    for u in lag:
        values(u, jnp.maximum(qi - 1, 0))
    for u in lag:
        scores(u, qi)
    for u in range(n_units):
        probs(u, masked=True)
        values(u, qi, masked=True)

    o_t = jnp.concatenate(
        [jnp.concatenate([acc_ref[u] * (1.0 / l_ref[u]) for u in reversed(range(h * n_slices, (h + 1) * n_slices))],
                         axis=1) for h in range(2)], axis=0)
    o_ref[...] = (o_t.T * _silu(gate_ref[...].astype(F32))).astype(BF16)


def _mla_attn(qt, k, vt, gate, batch):
    T = k.shape[0]
    tm = qt.shape[2]
    tq = ATTN_TILE
    qc = ATTN_UNIT_COLS
    S = T // batch
    n_q = S // tq
    sub = tq // tm
    n_pairs = vt.shape[1] // LANES
    n_units = 2 * (tq // qc)
    stat = pltpu.VMEM((n_units, 1, qc), F32)
    return pl.pallas_call(
        functools.partial(_attn_kernel, tm=tm),
        grid=(batch, n_pairs, n_q),
        in_specs=[pl.BlockSpec((sub, 2 * HEAD_LANES, tm), lambda b, p, i: (b * n_q + i, p, 0)),
                  pl.BlockSpec((S, 2 * HEAD_LANES), lambda b, p, i: (b, p)),
                  pl.BlockSpec((S // tm, LANES, tm), lambda b, p, i: (b, p, 0)),
                  pl.BlockSpec((tq, LANES), lambda b, p, i: (b * n_q + i, p))],
        out_specs=pl.BlockSpec((tq, LANES), lambda b, p, i: (b * n_q + i, p)),
        out_shape=jax.ShapeDtypeStruct((T, vt.shape[1]), BF16),
        scratch_shapes=([pltpu.VMEM((tq, qc), F32)] * n_units + [pltpu.VMEM((tq, qc), BF16)] * n_units
                        + [stat, stat, stat, stat, pltpu.VMEM((n_units, V_HEAD, qc), F32)]),
        compiler_params=_params("arbitrary", "arbitrary", "arbitrary"),
        name="mla_attn",
    )(qt, k, vt, gate)


def _mla_out_kernel(o_ref, h_ref, w_ref, g_ref, out_ref):
    h = h_ref[...] + _dot(o_ref[...], w_ref[...])
    out_ref[...] = _rms_scale(h) * g_ref[...]


def _mla_out(o, h1, w, g, tm):
    T, D = h1.shape
    full = lambda a: pl.BlockSpec(a.shape, lambda i: (0, 0))
    rows = lambda n: pl.BlockSpec((tm, n), lambda i: (i, 0))
    return pl.pallas_call(
        _mla_out_kernel,
        grid=(T // tm,),
        in_specs=[rows(o.shape[1]), rows(D), full(w), full(g)],
        out_specs=rows(D),
        out_shape=jax.ShapeDtypeStruct((T, D), F32),
        compiler_params=_params("arbitrary"),
        name="mla_out",
    )(o, h1, w, g)


def _pad_cols(a, n):
    return jnp.pad(a, ((0, 0), (0, n - a.shape[1])))


def _head_slots(w_nope, w_rope):
    k, h = w_nope.shape[0], w_nope.shape[1]
    pad = jnp.zeros((k, h, HEAD_LANES - QK_NOPE - QK_ROPE), w_nope.dtype)
    return jnp.concatenate([w_nope, w_rope, pad], axis=-1).reshape(k, h * HEAD_LANES)


def _rotate_half_cols(w_rope):
    w1, w2 = jnp.split(w_rope, 2, axis=-1)
    return jnp.concatenate([-w2, w1], axis=-1)


def kernel(x, positions, g_pre, ssm_w_in, ssm_conv_w, ssm_conv_b, ssm_dt_bias, ssm_A_log, ssm_D, ssm_g_out,
           ssm_w_out, kv_g_in, kv_w_down, kv_g_latent, kv_w_up, mla_w_in, mla_g_q, mla_w_uq, mla_w_out, g_final):
    B, S, D = x.shape
    T = B * S
    d_inner = ssm_w_out.shape[1]
    n_heads = ssm_dt_bias.shape[1]
    conv_dim = ssm_conv_w.shape[2]
    kv_lora = kv_g_latent.shape[0]
    q_lora = mla_g_q.shape[1]
    H = MLA_HEADS
    assert d_inner == n_heads * SSM_HEAD_DIM and n_heads <= LANES
    assert S % SSD_CHUNK == 0 and ssm_conv_w.shape[1] - 1 <= CONV_HALO
    x2d = x.reshape(T, D)

    w_in = _pad_cols(ssm_w_in[0], d_inner + conv_dim + LANES).astype(BF16)
    z, xbc, dt_raw = _ssm_in_proj(x2d, g_pre[0][None, :], w_in, d_inner, conv_dim, tm=512)

    head_of_channel = jnp.arange(d_inner) // SSM_HEAD_DIM
    expand = (jnp.arange(LANES)[:, None] == head_of_channel[None, :]).astype(BF16)
    e2 = jnp.concatenate([expand, expand], axis=0)
    t_idx = jnp.arange(SSD_CHUNK)
    kw = ssm_conv_w.shape[1]
    shift = jnp.concatenate([(t_idx[:, None] - s == t_idx[None, :]) for s in range(1, kw)], axis=0).astype(BF16)
    h1 = _ssd_mixer(
        z, xbc, dt_raw, x2d, ssm_conv_w[0], ssm_conv_b[0][None, :],
        _pad_cols(ssm_dt_bias[0][None, :], LANES), _pad_cols(ssm_A_log[0][None, :], LANES),
        jnp.repeat(ssm_D[0], SSM_HEAD_DIM)[None, :], ssm_g_out[0][None, :], ssm_w_out[0].astype(BF16), e2, shift,
        batch=B)

    invf = (ROPE_BASE ** (-jnp.arange(0, QK_ROPE, 2, dtype=F32) / QK_ROPE))[:, None]
    wdc = kv_w_down[:, :kv_lora].astype(BF16)
    wr = kv_w_down[:, kv_lora:]
    slot1 = lambda w: jnp.pad(w, ((0, 0), (QK_NOPE, HEAD_LANES - QK_NOPE - QK_ROPE)))
    wdr = jnp.concatenate([slot1(wr), slot1(_rotate_half_cols(wr))], axis=1).astype(BF16)
    wup = kv_w_up.reshape(kv_lora, H, QK_NOPE + V_HEAD)
    wuk = _head_slots(wup[:, :, :QK_NOPE], jnp.zeros((kv_lora, H, QK_ROPE), F32)).astype(BF16)
    wuvt = wup[:, :, QK_NOPE:].reshape(kv_lora, H * V_HEAD).T.astype(BF16)
    w_in_b = mla_w_in[0]
    wcq = w_in_b[:, :q_lora].astype(BF16)
    wgate = w_in_b[:, q_lora:].astype(BF16)
    wq = mla_w_uq[0].reshape(q_lora, H, QK_NOPE + QK_ROPE)
    wq_nope, wq_rope = wq[:, :, :QK_NOPE], wq[:, :, QK_NOPE:]
    wuqt = _head_slots(wq_nope, wq_rope).T.astype(BF16)
    scale = float((QK_NOPE + QK_ROPE) ** -0.5 * math.log2(math.e))

    qt, k, vt, gate = _mla_proj(
        h1, positions, invf, kv_g_in[None, :], g_pre[1][None, :], wdc, wdr, kv_g_latent[None, :],
        wuk, wuvt, wcq, wgate, mla_g_q[0][None, :], wuqt, tm=512, scale=scale)
    o = _mla_attn(qt, k, vt, gate, batch=B)
    out = _mla_out(o, h1, mla_w_out[0].astype(BF16), g_final[None, :], tm=1024)
    return out.reshape(B, S, D)
```

```python
import functools
import math

import jax
import jax.numpy as jnp
from jax import lax
from jax.experimental import pallas as pl
from jax.experimental.pallas import tpu as pltpu

F32 = jnp.float32
BF16 = jnp.bfloat16

SSM_HEAD_DIM = 64
SSM_GROUPS = 4
SSM_STATE = 128
MLA_HEADS = 16
QK_NOPE = 64
QK_ROPE = 32
V_HEAD = 64
ROPE_BASE = 10000.0
EPS = 1e-6
LOG2_E = math.log2(math.e)

LANES = 128
BF16_SUBLANES = 16
HEAD_LANES = 128
SSD_CHUNK = 256
CONV_HALO = 8
PROJ_ROWS = 512
OUT_ROWS = 1024
VMEM_LIMIT = 56 * 1024 * 1024


def _params(*sem):
    return pltpu.CompilerParams(dimension_semantics=sem, vmem_limit_bytes=VMEM_LIMIT)


def _silu(x):
    return x * (1.0 / (1.0 + jnp.exp(-x)))


def _softplus(x):
    return jnp.maximum(x, 0.0) + jnp.log1p(jnp.exp(-jnp.abs(x)))


def _rms_scale(x):
    return x * lax.rsqrt(jnp.mean(x * x, axis=-1, keepdims=True) + EPS)


def _dot(a, b):
    return jnp.dot(a, b, preferred_element_type=F32)


def _split2(x):
    hi = x.astype(BF16)
    lo = (x - hi.astype(F32)).astype(BF16)
    return jnp.concatenate([hi, lo], axis=-1)


def _ssm_in_proj_kernel(x_ref, g_ref, w_ref, z_ref, xbc_ref, dt_ref):
    nz = z_ref.shape[1]
    nx = xbc_ref.shape[1]
    hn = (_rms_scale(x_ref[...]) * g_ref[...]).astype(BF16)
    z_ref[...] = _dot(hn, w_ref[:, :nz]).astype(BF16)
    xbc_ref[...] = _dot(hn, w_ref[:, nz:nz + nx]).astype(BF16)
    dt_ref[...] = _dot(hn, w_ref[:, nz + nx:])


def _ssm_in_proj(x2d, g, w, d_inner, conv_dim, tm):
    T, D = x2d.shape
    n_dt = w.shape[1] - d_inner - conv_dim
    full = lambda a: pl.BlockSpec(a.shape, lambda i: (0, 0))
    rows = lambda n: pl.BlockSpec((tm, n), lambda i: (i, 0))
    return pl.pallas_call(
        _ssm_in_proj_kernel,
        grid=(T // tm,),
        in_specs=[rows(D), full(g), full(w)],
        out_specs=[rows(d_inner), rows(conv_dim), rows(n_dt)],
        out_shape=[jax.ShapeDtypeStruct((T, d_inner), BF16),
                   jax.ShapeDtypeStruct((T, conv_dim), BF16),
                   jax.ShapeDtypeStruct((T, n_dt), F32)],
        compiler_params=_params("arbitrary"),
        name="ssm_in_proj",
    )(x2d, g, w)


def _ssd_kernel(z_ref, xbc_ref, dt_ref, x_ref, convw_ref, convb_ref, dtb_ref, alog_ref, dskip_ref,
                gout_ref, wout_ref, e2_ref, shift_ref, o_ref, hbuf, state, ybuf, *, d_inner, n_groups, d_state):
    Q = SSD_CHUNK
    gw = d_inner // n_groups
    heads_per_group = gw // SSM_HEAD_DIM
    pairs_per_group = gw // LANES
    kw = convw_ref.shape[0]

    @pl.when(pl.program_id(1) == 0)
    def _():
        hbuf[0:CONV_HALO, :] = jnp.zeros((CONV_HALO, hbuf.shape[1]), F32)
        state[...] = jnp.zeros(state.shape, F32)

    u_bf = xbc_ref[...]
    u = u_bf.astype(F32)
    shifted = _dot(shift_ref[...], u_bf)
    conv = convb_ref[...] + convw_ref[kw - 1:kw, :] * u
    for s in range(1, kw):
        conv = conv + convw_ref[kw - 1 - s:kw - s, :] * shifted[(s - 1) * Q:s * Q, :]
    hbuf[CONV_HALO:2 * CONV_HALO, :] = u[0:CONV_HALO, :]
    head = convb_ref[...]
    for k in range(kw):
        off = CONV_HALO - (kw - 1 - k)
        head = head + convw_ref[k:k + 1, :] * hbuf[off:off + CONV_HALO, :]
    hbuf[0:CONV_HALO, :] = u[Q - CONV_HALO:Q, :]
    conv = jnp.concatenate([head, conv[CONV_HALO:, :]], axis=0)
    xbc = _silu(conv)
    xs = xbc[:, :d_inner]

    dt = _softplus(dt_ref[...] + dtb_ref[...])
    dA = dt * (-jnp.exp(alog_ref[...]))
    row = lax.broadcasted_iota(jnp.int32, (Q, Q), 0)
    col = lax.broadcasted_iota(jnp.int32, (Q, Q), 1)
    causal = row >= col
    tril = causal.astype(BF16)
    hi = dA.astype(BF16)
    mid = (dA - hi.astype(F32)).astype(BF16)
    lo = (dA - hi.astype(F32) - mid.astype(F32)).astype(BF16)
    cum = (_dot(tril, hi) + _dot(tril, mid) + _dot(tril, lo)) * LOG2_E
    cum_t = cum.T
    cum_end = cum[Q - 1:Q, :]

    e2 = e2_ref[...]
    dt_x = _dot(_split2(dt), e2)
    ecum_x = _dot(_split2(jnp.exp2(cum)), e2)
    wend_x = _dot(_split2(jnp.exp2(cum_end - cum) * dt), e2)
    edec_x = _dot(_split2(jnp.broadcast_to(jnp.exp2(cum_end), (8, LANES))), e2)[0:1, :]

    lane = lax.broadcasted_iota(jnp.int32, (Q, LANES), 1)
    first_half = lane < SSM_HEAD_DIM
    xdt = xs * dt_x

    for g in range(n_groups):
        b_g = xbc[:, d_inner + g * d_state:d_inner + (g + 1) * d_state].astype(BF16)
        c_off = d_inner + n_groups * d_state
        c_g = xbc[:, c_off + g * d_state:c_off + (g + 1) * d_state].astype(BF16)
        cb = lax.dot_general(c_g, b_g, (((1,), (1,)), ((), ())), preferred_element_type=F32)
        gs = slice(g * gw, (g + 1) * gw)
        st = state[g]
        y_inter = _dot(c_g, st.astype(BF16)) * ecum_x[:, gs]
        xw = (xs[:, gs] * wend_x[:, gs]).astype(BF16)
        upd = lax.dot_general(b_g, xw, (((0,), (0,)), ((), ())), preferred_element_type=F32)
        state[g] = st * edec_x[:, gs] + upd
        for p in range(pairs_per_group):
            c0 = g * gw + p * LANES
            x_pair = xdt[:, c0:c0 + LANES]
            acc = y_inter[:, p * LANES:(p + 1) * LANES]
            for hh in range(2):
                h = g * heads_per_group + 2 * p + hh
                seg = cum[:, h:h + 1] - cum_t[h:h + 1, :]
                m = jnp.where(causal, cb * jnp.exp2(seg), 0.0).astype(BF16)
                keep = first_half if hh == 0 else jnp.logical_not(first_half)
                x_h = jnp.where(keep, x_pair, 0.0).astype(BF16)
                acc = acc + _dot(m, x_h)
            ybuf[:, c0:c0 + LANES] = acc

    y = (ybuf[...] + dskip_ref[...] * xs) * _silu(z_ref[...].astype(F32))
    for g in range(n_groups):
        gs = slice(g * gw, (g + 1) * gw)
        ybuf[:, gs] = _rms_scale(y[:, gs])
    yn = (ybuf[...] * gout_ref[...]).astype(BF16)
    o_ref[...] = x_ref[...] + _dot(yn, wout_ref[...])


def _ssd_mixer(z, xbc, dt_raw, x2d, convw, convb, dtb, alog, dskip, gout, wout, e2, shift, batch):
    T, d_inner = z.shape
    D = x2d.shape[1]
    conv_dim = xbc.shape[1]
    n_chunks = T // batch // SSD_CHUNK
    gw = d_inner // SSM_GROUPS
    full = lambda a: pl.BlockSpec(a.shape, lambda b, c: (0, 0))
    rows = lambda n: pl.BlockSpec((SSD_CHUNK, n), lambda b, c: (b * n_chunks + c, 0))
    kernel = functools.partial(_ssd_kernel, d_inner=d_inner, n_groups=SSM_GROUPS, d_state=SSM_STATE)
    return pl.pallas_call(
        kernel,
        grid=(batch, n_chunks),
        in_specs=[rows(d_inner), rows(conv_dim), rows(LANES), rows(D), full(convw), full(convb), full(dtb),
                  full(alog), full(dskip), full(gout), full(wout), full(e2), full(shift)],
        out_specs=rows(D),
        out_shape=jax.ShapeDtypeStruct((T, D), F32),
        scratch_shapes=[pltpu.VMEM((2 * CONV_HALO, conv_dim), F32),
                        pltpu.VMEM((SSM_GROUPS, SSM_STATE, gw), F32),
                        pltpu.VMEM((SSD_CHUNK, d_inner), F32)],
        compiler_params=_params("arbitrary", "arbitrary"),
        name="ssd_mixer",
    )(z, xbc, dt_raw, x2d, convw, convb, dtb, alog, dskip, gout, wout, e2, shift)


def _nt_dot(a, b):
    return lax.dot_general(a, b, (((1,), (1,)), ((), ())), preferred_element_type=F32)


def _mla_proj_kernel(h_ref, pos_ref, invf_ref, gkv_ref, gq_ref, wdc_ref, wdr_ref,
                     gckv_ref, wuk_ref, wuvt_ref, wcq_ref, wgate_ref, gqn_ref, wuqt_ref,
                     qt_ref, k_ref, vt_ref, gate_ref, *, scale):
    tm = h_ref.shape[0]
    half = QK_ROPE // 2
    hs = _rms_scale(h_ref[...])
    hkv = (hs * gkv_ref[...]).astype(BF16)
    hq = (hs * gq_ref[...]).astype(BF16)

    ang = invf_ref[...] * pos_ref[...].astype(F32)
    cos = jnp.cos(ang)
    sin = jnp.sin(ang)

    ones = jnp.ones((QK_NOPE, tm), F32)
    tail = HEAD_LANES - QK_NOPE - QK_ROPE
    cos_k = jnp.concatenate([ones, cos, cos, ones[:tail]], axis=0).T
    sin_k = jnp.concatenate([0.0 * ones, sin, sin, 0.0 * ones[:tail]], axis=0).T
    ckv = (_rms_scale(_dot(hkv, wdc_ref[...])) * gckv_ref[...]).astype(BF16)
    kr2 = _dot(hkv, wdr_ref[...])
    k_rope = kr2[:, :HEAD_LANES] * cos_k + kr2[:, HEAD_LANES:] * sin_k
    vt_ref[0] = _nt_dot(wuvt_ref[...], ckv).astype(BF16)
    k_nope = _dot(ckv, wuk_ref[...])

    gate_ref[...] = _dot(hq, wgate_ref[...]).astype(BF16)
    cq = (_rms_scale(_dot(hq, wcq_ref[...])) * gqn_ref[...]).astype(BF16)
    qm = _nt_dot(wuqt_ref[...], cq)
    cos_q = cos * scale
    sin_q = sin * scale
    for h in range(k_ref.shape[1] // HEAD_LANES):
        r0 = h * HEAD_LANES
        r1 = r0 + QK_NOPE
        r2 = r1 + half
        r3 = r2 + half
        k_ref[:, r0:r0 + HEAD_LANES] = (k_nope[:, r0:r0 + HEAD_LANES] + k_rope).astype(BF16)
        x1 = qm[r1:r2, :]
        x2 = qm[r2:r3, :]
        qt_ref[0, r0:r1, :] = (qm[r0:r1, :] * scale).astype(BF16)
        qt_ref[0, r1:r2, :] = (x1 * cos_q - x2 * sin_q).astype(BF16)
        qt_ref[0, r2:r3, :] = (x1 * sin_q + x2 * cos_q).astype(BF16)
        qt_ref[0, r3:r0 + HEAD_LANES, :] = qm[r3:r0 + HEAD_LANES, :].astype(BF16)


def _mla_proj(h1, pos, invf, gkv, gq, wdc, wdr, gckv, wuk, wuvt, wcq, wgate, gqn, wuqt, tm, scale):
    T, D = h1.shape
    full = lambda a: pl.BlockSpec(a.shape, lambda i: (0, 0))
    rows = lambda n: pl.BlockSpec((tm, n), lambda i: (i, 0))
    cols = lambda n: pl.BlockSpec((1, n, tm), lambda i: (i, 0, 0))
    hq = wuqt.shape[0]
    hv = wuvt.shape[0]
    hg = wgate.shape[1]
    weights = (invf, gkv, gq, wdc, wdr, gckv, wuk, wuvt, wcq, wgate, gqn, wuqt)
    return pl.pallas_call(
        functools.partial(_mla_proj_kernel, scale=scale),
        grid=(T // tm,),
        in_specs=[rows(D), pl.BlockSpec((1, tm), lambda i: (0, i))] + [full(w) for w in weights],
        out_specs=[cols(hq), rows(hq), cols(hv), rows(hg)],
        out_shape=[jax.ShapeDtypeStruct((T // tm, hq, tm), BF16), jax.ShapeDtypeStruct((T, hq), BF16),
                   jax.ShapeDtypeStruct((T // tm, hv, tm), BF16), jax.ShapeDtypeStruct((T, hg), BF16)],
        compiler_params=_params("arbitrary"),
        name="mla_proj",
    )(h1, pos.reshape(1, T), *weights)


ATTN_TILE = 1024
ATTN_UNIT_COLS = 256
ATTN_LAG_UNITS = 1


def _attn_kernel(qt_ref, k_ref, vt_ref, gate_ref, o_ref, *scratch, tm):
    tq = ATTN_TILE
    qc = ATTN_UNIT_COLS
    n_slices = tq // qc
    n_units = 2 * n_slices
    sub = tq // tm
    s_refs = scratch[:n_units]
    p_refs = scratch[n_units:2 * n_units]
    mx_ref, m_ref, alpha_ref, l_ref, acc_ref = scratch[2 * n_units:]
    qi = pl.program_id(2)
    lead = range(n_units - ATTN_LAG_UNITS)
    lag = range(n_units - ATTN_LAG_UNITS, n_units)

    m_ref[...] = jnp.full(m_ref.shape, -jnp.inf, F32)
    l_ref[...] = jnp.zeros(l_ref.shape, F32)
    acc_ref[...] = jnp.zeros(acc_ref.shape, F32)
    for u in lag:
        p_refs[u][...] = jnp.zeros(p_refs[u].shape, BF16)
    alpha_ref[...] = jnp.ones(alpha_ref.shape, F32)

    def head_slice(u):
        h, r = divmod(u, n_slices)
        return h, n_slices - 1 - r

    def scores(u, j):
        h, c = head_slice(u)
        hs = slice(h * HEAD_LANES, (h + 1) * HEAD_LANES)
        ks = pl.ds(pl.multiple_of(j * tq, tq), tq)
        t, off = divmod(c * qc, tm)
        s = _dot(k_ref[ks, hs], qt_ref[t, hs, off:off + qc])
        s_refs[u][...] = s
        mx_ref[u] = jnp.max(s, axis=0, keepdims=True)

    def diag_keys(u):
        return (head_slice(u)[1] + 1) * qc

    def probs(u, masked):
        c = head_slice(u)[1]
        nk = diag_keys(u) if masked else tq
        s = s_refs[u][0:nk, :]
        if masked:
            key = lax.broadcasted_iota(jnp.int32, (nk, qc), 0)
            qry = lax.broadcasted_iota(jnp.int32, (nk, qc), 1) + c * qc
            s = jnp.where(key <= qry, s, -jnp.inf)
            mx = jnp.max(s, axis=0, keepdims=True)
        else:
            mx = mx_ref[u]
        m_old = m_ref[u]
        m_new = jnp.maximum(m_old, mx)
        alpha_ref[u] = jnp.exp2(m_old - m_new)
        m_ref[u] = m_new
        p_refs[u][0:nk, :] = jnp.exp2(s - m_new).astype(BF16)

    def values(u, j, masked=False):
        h = u // n_slices
        vs = slice(h * V_HEAD, (h + 1) * V_HEAD)
        nk = diag_keys(u) if masked else tq
        pv = None
        for t in range(sub):
            n = min(nk - t * tm, tm)
            if n <= 0:
                break
            lhs = jnp.concatenate([vt_ref[j * sub + t, vs, 0:n], jnp.ones((BF16_SUBLANES, n), BF16)], axis=0)
            part = _dot(lhs, p_refs[u][t * tm:t * tm + n, :])
            pv = part if pv is None else pv + part
        alpha = alpha_ref[u]
        acc_ref[u] = alpha * acc_ref[u] + pv[:V_HEAD, :]
        l_ref[u] = alpha * l_ref[u] + pv[V_HEAD:V_HEAD + 1, :]

    for u in lead:
        scores(u, 0)

    def body(j, _):
        for u in lag:
            values(u, jnp.maximum(j - 1, 0))
        for u in lag:
            scores(u, j)
        for u in lead:
            probs(u, masked=False)
            values(u, j)
            scores(u, j + 1)
        for u in lag:
            probs(u, masked=False)
        return 0

    lax.fori_loop(0, qi, body, 0)
    for u in lag:
        values(u, jnp.maximum(qi - 1, 0))
    for u in lag:
        scores(u, qi)
    for u in range(n_units):
        probs(u, masked=True)
        values(u, qi, masked=True)

    o_t = jnp.concatenate(
        [jnp.concatenate([acc_ref[u] * (1.0 / l_ref[u]) for u in reversed(range(h * n_slices, (h + 1) * n_slices))],
                         axis=1) for h in range(2)], axis=0)
    o_ref[...] = (o_t.T * _silu(gate_ref[...].astype(F32))).astype(BF16)


def _mla_attn(qt, k, vt, gate, batch):
    T = k.shape[0]
    tm = qt.shape[2]
    tq = ATTN_TILE
    qc = ATTN_UNIT_COLS
    S = T // batch
    n_q = S // tq
    sub = tq // tm
    n_pairs = vt.shape[1] // LANES
    n_units = 2 * (tq // qc)
    stat = pltpu.VMEM((n_units, 1, qc), F32)
    return pl.pallas_call(
        functools.partial(_attn_kernel, tm=tm),
        grid=(batch, n_pairs, n_q),
        in_specs=[pl.BlockSpec((sub, 2 * HEAD_LANES, tm), lambda b, p, i: (b * n_q + i, p, 0)),
                  pl.BlockSpec((S, 2 * HEAD_LANES), lambda b, p, i: (b, p)),
                  pl.BlockSpec((S // tm, LANES, tm), lambda b, p, i: (b, p, 0)),
                  pl.BlockSpec((tq, LANES), lambda b, p, i: (b * n_q + i, p))],
        out_specs=pl.BlockSpec((tq, LANES), lambda b, p, i: (b * n_q + i, p)),
        out_shape=jax.ShapeDtypeStruct((T, vt.shape[1]), BF16),
        scratch_shapes=([pltpu.VMEM((tq, qc), F32)] * n_units + [pltpu.VMEM((tq, qc), BF16)] * n_units
                        + [stat, stat, stat, stat, pltpu.VMEM((n_units, V_HEAD, qc), F32)]),
        compiler_params=_params("arbitrary", "arbitrary", "arbitrary"),
        name="mla_attn",
    )(qt, k, vt, gate)


def _mla_out_kernel(o_ref, h_ref, w_ref, g_ref, out_ref):
    h = h_ref[...] + _dot(o_ref[...], w_ref[...])
    out_ref[...] = _rms_scale(h) * g_ref[...]


def _mla_out(o, h1, w, g, tm):
    T, D = h1.shape
    full = lambda a: pl.BlockSpec(a.shape, lambda i: (0, 0))
    rows = lambda n: pl.BlockSpec((tm, n), lambda i: (i, 0))
    return pl.pallas_call(
        _mla_out_kernel,
        grid=(T // tm,),
        in_specs=[rows(o.shape[1]), rows(D), full(w), full(g)],
        out_specs=rows(D),
        out_shape=jax.ShapeDtypeStruct((T, D), F32),
        compiler_params=_params("arbitrary"),
        name="mla_out",
    )(o, h1, w, g)


def _pad_cols(a, n):
    return jnp.pad(a, ((0, 0), (0, n - a.shape[1])))


def _head_slots(w_nope, w_rope):
    k, h = w_nope.shape[0], w_nope.shape[1]
    pad = jnp.zeros((k, h, HEAD_LANES - QK_NOPE - QK_ROPE), w_nope.dtype)
    return jnp.concatenate([w_nope, w_rope, pad], axis=-1).reshape(k, h * HEAD_LANES)


def _rotate_half_cols(w_rope):
    w1, w2 = jnp.split(w_rope, 2, axis=-1)
    return jnp.concatenate([-w2, w1], axis=-1)


def kernel(x, positions, g_pre, ssm_w_in, ssm_conv_w, ssm_conv_b, ssm_dt_bias, ssm_A_log, ssm_D, ssm_g_out,
           ssm_w_out, kv_g_in, kv_w_down, kv_g_latent, kv_w_up, mla_w_in, mla_g_q, mla_w_uq, mla_w_out, g_final):
    B, S, D = x.shape
    T = B * S
    d_inner = ssm_w_out.shape[1]
    n_heads = ssm_dt_bias.shape[1]
    conv_dim = ssm_conv_w.shape[2]
    kv_lora = kv_g_latent.shape[0]
    q_lora = mla_g_q.shape[1]
    H = MLA_HEADS
    assert d_inner == n_heads * SSM_HEAD_DIM and n_heads <= LANES
    assert S % SSD_CHUNK == 0 and ssm_conv_w.shape[1] - 1 <= CONV_HALO
    assert S % ATTN_TILE == 0 and ATTN_TILE % PROJ_ROWS == 0 and T % OUT_ROWS == 0
    assert mla_w_out.shape[1] == H * V_HEAD and H % 2 == 0
    x2d = x.reshape(T, D)

    w_in = _pad_cols(ssm_w_in[0], d_inner + conv_dim + LANES).astype(BF16)
    z, xbc, dt_raw = _ssm_in_proj(x2d, g_pre[0][None, :], w_in, d_inner, conv_dim, tm=PROJ_ROWS)

    head_of_channel = jnp.arange(d_inner) // SSM_HEAD_DIM
    expand = (jnp.arange(LANES)[:, None] == head_of_channel[None, :]).astype(BF16)
    e2 = jnp.concatenate([expand, expand], axis=0)
    t_idx = jnp.arange(SSD_CHUNK)
    kw = ssm_conv_w.shape[1]
    shift = jnp.concatenate([(t_idx[:, None] - s == t_idx[None, :]) for s in range(1, kw)], axis=0).astype(BF16)
    h1 = _ssd_mixer(
        z, xbc, dt_raw, x2d, ssm_conv_w[0], ssm_conv_b[0][None, :],
        _pad_cols(ssm_dt_bias[0][None, :], LANES), _pad_cols(ssm_A_log[0][None, :], LANES),
        jnp.repeat(ssm_D[0], SSM_HEAD_DIM)[None, :], ssm_g_out[0][None, :], ssm_w_out[0].astype(BF16), e2, shift,
        batch=B)

    invf = (ROPE_BASE ** (-jnp.arange(0, QK_ROPE, 2, dtype=F32) / QK_ROPE))[:, None]
    wdc = kv_w_down[:, :kv_lora].astype(BF16)
    wr = kv_w_down[:, kv_lora:]
    slot1 = lambda w: jnp.pad(w, ((0, 0), (QK_NOPE, HEAD_LANES - QK_NOPE - QK_ROPE)))
    wdr = jnp.concatenate([slot1(wr), slot1(_rotate_half_cols(wr))], axis=1).astype(BF16)
    wup = kv_w_up.reshape(kv_lora, H, QK_NOPE + V_HEAD)
    wuk = _head_slots(wup[:, :, :QK_NOPE], jnp.zeros((kv_lora, H, QK_ROPE), F32)).astype(BF16)
    wuvt = wup[:, :, QK_NOPE:].reshape(kv_lora, H * V_HEAD).T.astype(BF16)
    w_in_b = mla_w_in[0]
    wcq = w_in_b[:, :q_lora].astype(BF16)
    wgate = w_in_b[:, q_lora:].astype(BF16)
    wq = mla_w_uq[0].reshape(q_lora, H, QK_NOPE + QK_ROPE)
    wq_nope, wq_rope = wq[:, :, :QK_NOPE], wq[:, :, QK_NOPE:]
    wuqt = _head_slots(wq_nope, wq_rope).T.astype(BF16)
    scale = float((QK_NOPE + QK_ROPE) ** -0.5 * math.log2(math.e))

    qt, k, vt, gate = _mla_proj(
        h1, positions, invf, kv_g_in[None, :], g_pre[1][None, :], wdc, wdr, kv_g_latent[None, :],
        wuk, wuvt, wcq, wgate, mla_g_q[0][None, :], wuqt, tm=PROJ_ROWS, scale=scale)
    o = _mla_attn(qt, k, vt, gate, batch=B)
    out = _mla_out(o, h1, mla_w_out[0].astype(BF16), g_final[None, :], tm=OUT_ROWS)
    return out.reshape(B, S, D)
```

```python
import functools
import math

import jax
import jax.numpy as jnp
from jax import lax
from jax.experimental import pallas as pl
from jax.experimental.pallas import tpu as pltpu

F32 = jnp.float32
BF16 = jnp.bfloat16

SSM_HEAD_DIM = 64
SSM_GROUPS = 4
SSM_STATE = 128
MLA_HEADS = 16
QK_NOPE = 64
QK_ROPE = 32
V_HEAD = 64
ROPE_BASE = 10000.0
EPS = 1e-6
LOG2_E = math.log2(math.e)

LANES = 128
BF16_SUBLANES = 16
HEAD_LANES = 128
SSD_CHUNK = 256
CONV_HALO = 8
PROJ_ROWS = 512
OUT_ROWS = 1024
VMEM_LIMIT = 56 * 1024 * 1024


def _params(*sem):
    return pltpu.CompilerParams(dimension_semantics=sem, vmem_limit_bytes=VMEM_LIMIT)


def _silu(x):
    return x * (1.0 / (1.0 + jnp.exp(-x)))


def _softplus(x):
    return jnp.maximum(x, 0.0) + jnp.log1p(jnp.exp(-jnp.abs(x)))


def _rms_scale(x):
    return x * lax.rsqrt(jnp.mean(x * x, axis=-1, keepdims=True) + EPS)


def _dot(a, b):
    return jnp.dot(a, b, preferred_element_type=F32)


def _nt_dot(a, b):
    return lax.dot_general(a, b, (((1,), (1,)), ((), ())), preferred_element_type=F32)


def _split2(x):
    hi = x.astype(BF16)
    lo = (x - hi.astype(F32)).astype(BF16)
    return jnp.concatenate([hi, lo], axis=-1)


def _ssm_in_proj_kernel(x_ref, g_ref, wt_ref, z_ref, xbc_ref, dt_ref):
    nz = z_ref.shape[1]
    nx = xbc_ref.shape[1]
    hn = (_rms_scale(x_ref[...]) * g_ref[...]).astype(BF16)
    z_ref[...] = _nt_dot(hn, wt_ref[:nz, :]).astype(BF16)
    xbc_ref[...] = _nt_dot(hn, wt_ref[nz:nz + nx, :]).astype(BF16)
    dt_ref[...] = _nt_dot(hn, wt_ref[nz + nx:, :])


def _ssm_in_proj(x2d, g, w, d_inner, conv_dim, tm):
    T, D = x2d.shape
    n_dt = w.shape[0] - d_inner - conv_dim
    full = lambda a: pl.BlockSpec(a.shape, lambda i: (0, 0))
    rows = lambda n: pl.BlockSpec((tm, n), lambda i: (i, 0))
    return pl.pallas_call(
        _ssm_in_proj_kernel,
        grid=(T // tm,),
        in_specs=[rows(D), full(g), full(w)],
        out_specs=[rows(d_inner), rows(conv_dim), rows(n_dt)],
        out_shape=[jax.ShapeDtypeStruct((T, d_inner), BF16),
                   jax.ShapeDtypeStruct((T, conv_dim), BF16),
                   jax.ShapeDtypeStruct((T, n_dt), F32)],
        compiler_params=_params("arbitrary"),
        name="ssm_in_proj",
    )(x2d, g, w)


def _ssd_kernel(z_ref, xbc_ref, dt_ref, x_ref, convw_ref, convb_ref, dtb_ref, alog_ref, dskip_ref,
                gout_ref, wout_ref, e2_ref, shift_ref, o_ref, hbuf, state, ybuf, *, d_inner, n_groups, d_state):
    Q = SSD_CHUNK
    gw = d_inner // n_groups
    heads_per_group = gw // SSM_HEAD_DIM
    pairs_per_group = gw // LANES
    kw = convw_ref.shape[0]

    @pl.when(pl.program_id(1) == 0)
    def _():
        hbuf[0:CONV_HALO, :] = jnp.zeros((CONV_HALO, hbuf.shape[1]), F32)
        state[...] = jnp.zeros(state.shape, F32)

    u_bf = xbc_ref[...]
    u = u_bf.astype(F32)
    shifted = _dot(shift_ref[...], u_bf)
    conv = convb_ref[...] + convw_ref[kw - 1:kw, :] * u
    for s in range(1, kw):
        conv = conv + convw_ref[kw - 1 - s:kw - s, :] * shifted[(s - 1) * Q:s * Q, :]
    hbuf[CONV_HALO:2 * CONV_HALO, :] = u[0:CONV_HALO, :]
    head = convb_ref[...]
    for k in range(kw):
        off = CONV_HALO - (kw - 1 - k)
        head = head + convw_ref[k:k + 1, :] * hbuf[off:off + CONV_HALO, :]
    hbuf[0:CONV_HALO, :] = u[Q - CONV_HALO:Q, :]
    conv = jnp.concatenate([head, conv[CONV_HALO:, :]], axis=0)
    xbc = _silu(conv)
    xs = xbc[:, :d_inner]

    dt = _softplus(dt_ref[...] + dtb_ref[...])
    dA = dt * (-jnp.exp(alog_ref[...]))
    row = lax.broadcasted_iota(jnp.int32, (Q, Q), 0)
    col = lax.broadcasted_iota(jnp.int32, (Q, Q), 1)
    causal = row >= col
    tril = causal.astype(BF16)
    hi = dA.astype(BF16)
    mid = (dA - hi.astype(F32)).astype(BF16)
    lo = (dA - hi.astype(F32) - mid.astype(F32)).astype(BF16)
    cum = (_dot(tril, hi) + _dot(tril, mid) + _dot(tril, lo)) * LOG2_E
    cum_t = cum.T
    cum_end = cum[Q - 1:Q, :]

    e2 = e2_ref[...]
    dt_x = _dot(_split2(dt), e2)
    ecum_x = _dot(_split2(jnp.exp2(cum)), e2)
    wend_x = _dot(_split2(jnp.exp2(cum_end - cum) * dt), e2)
    edec_x = _dot(_split2(jnp.broadcast_to(jnp.exp2(cum_end), (8, LANES))), e2)[0:1, :]

    lane = lax.broadcasted_iota(jnp.int32, (Q, LANES), 1)
    first_half = lane < SSM_HEAD_DIM
    xdt = xs * dt_x

    for g in range(n_groups):
        b_g = xbc[:, d_inner + g * d_state:d_inner + (g + 1) * d_state].astype(BF16)
        c_off = d_inner + n_groups * d_state
        c_g = xbc[:, c_off + g * d_state:c_off + (g + 1) * d_state].astype(BF16)
        cb = lax.dot_general(c_g, b_g, (((1,), (1,)), ((), ())), preferred_element_type=F32)
        gs = slice(g * gw, (g + 1) * gw)
        st = state[g]
        y_inter = _dot(c_g, st.astype(BF16)) * ecum_x[:, gs]
        xw = (xs[:, gs] * wend_x[:, gs]).astype(BF16)
        upd = lax.dot_general(b_g, xw, (((0,), (0,)), ((), ())), preferred_element_type=F32)
        state[g] = st * edec_x[:, gs] + upd
        for p in range(pairs_per_group):
            c0 = g * gw + p * LANES
            x_pair = xdt[:, c0:c0 + LANES]
            acc = y_inter[:, p * LANES:(p + 1) * LANES]
            for hh in range(2):
                h = g * heads_per_group + 2 * p + hh
                seg = cum[:, h:h + 1] - cum_t[h:h + 1, :]
                m = jnp.where(causal, cb * jnp.exp2(seg), 0.0).astype(BF16)
                keep = first_half if hh == 0 else jnp.logical_not(first_half)
                x_h = jnp.where(keep, x_pair, 0.0).astype(BF16)
                acc = acc + _dot(m, x_h)
            ybuf[:, c0:c0 + LANES] = acc

    y = (ybuf[...] + dskip_ref[...] * xs) * _silu(z_ref[...].astype(F32))
    for g in range(n_groups):
        gs = slice(g * gw, (g + 1) * gw)
        ybuf[:, gs] = _rms_scale(y[:, gs])
    yn = (ybuf[...] * gout_ref[...]).astype(BF16)
    o_ref[...] = x_ref[...] + _dot(yn, wout_ref[...])


def _ssd_mixer(z, xbc, dt_raw, x2d, convw, convb, dtb, alog, dskip, gout, wout, e2, shift, batch):
    T, d_inner = z.shape
    D = x2d.shape[1]
    conv_dim = xbc.shape[1]
    n_chunks = T // batch // SSD_CHUNK
    gw = d_inner // SSM_GROUPS
    full = lambda a: pl.BlockSpec(a.shape, lambda b, c: (0, 0))
    rows = lambda n: pl.BlockSpec((SSD_CHUNK, n), lambda b, c: (b * n_chunks + c, 0))
    kernel = functools.partial(_ssd_kernel, d_inner=d_inner, n_groups=SSM_GROUPS, d_state=SSM_STATE)
    return pl.pallas_call(
        kernel,
        grid=(batch, n_chunks),
        in_specs=[rows(d_inner), rows(conv_dim), rows(LANES), rows(D), full(convw), full(convb), full(dtb),
                  full(alog), full(dskip), full(gout), full(wout), full(e2), full(shift)],
        out_specs=rows(D),
        out_shape=jax.ShapeDtypeStruct((T, D), F32),
        scratch_shapes=[pltpu.VMEM((2 * CONV_HALO, conv_dim), F32),
                        pltpu.VMEM((SSM_GROUPS, SSM_STATE, gw), F32),
                        pltpu.VMEM((SSD_CHUNK, d_inner), F32)],
        compiler_params=_params("arbitrary", "arbitrary"),
        name="ssd_mixer",
    )(z, xbc, dt_raw, x2d, convw, convb, dtb, alog, dskip, gout, wout, e2, shift)


def _mla_proj_kernel(h_ref, pos_ref, invf_ref, gkv_ref, gq_ref, wdc_ref, wdr_ref,
                     gckv_ref, wuk_ref, wuvt_ref, wcq_ref, wgate_ref, gqn_ref, wuqt_ref,
                     qt_ref, k_ref, vt_ref, gate_ref, *, scale):
    tm = h_ref.shape[0]
    half = QK_ROPE // 2
    hs = _rms_scale(h_ref[...])
    hkv = (hs * gkv_ref[...]).astype(BF16)
    hq = (hs * gq_ref[...]).astype(BF16)

    ang = invf_ref[...] * pos_ref[...].astype(F32)
    cos = jnp.cos(ang)
    sin = jnp.sin(ang)

    ones = jnp.ones((QK_NOPE, tm), F32)
    tail = HEAD_LANES - QK_NOPE - QK_ROPE
    cos_k = jnp.concatenate([ones, cos, cos, ones[:tail]], axis=0).T
    sin_k = jnp.concatenate([0.0 * ones, sin, sin, 0.0 * ones[:tail]], axis=0).T
    ckv = (_rms_scale(_dot(hkv, wdc_ref[...])) * gckv_ref[...]).astype(BF16)
    kr2 = _dot(hkv, wdr_ref[...])
    k_rope = kr2[:, :HEAD_LANES] * cos_k + kr2[:, HEAD_LANES:] * sin_k
    vt_ref[0] = _nt_dot(wuvt_ref[...], ckv).astype(BF16)
    k_nope = _dot(ckv, wuk_ref[...])

    gate_ref[...] = _dot(hq, wgate_ref[...]).astype(BF16)
    cq = (_rms_scale(_dot(hq, wcq_ref[...])) * gqn_ref[...]).astype(BF16)
    qm = _nt_dot(wuqt_ref[...], cq)
    cos_q = cos * scale
    sin_q = sin * scale
    for h in range(k_ref.shape[1] // HEAD_LANES):
        r0 = h * HEAD_LANES
        r1 = r0 + QK_NOPE
        r2 = r1 + half
        r3 = r2 + half
        k_ref[:, r0:r0 + HEAD_LANES] = (k_nope[:, r0:r0 + HEAD_LANES] + k_rope).astype(BF16)
        x1 = qm[r1:r2, :]
        x2 = qm[r2:r3, :]
        qt_ref[0, r0:r1, :] = (qm[r0:r1, :] * scale).astype(BF16)
        qt_ref[0, r1:r2, :] = (x1 * cos_q - x2 * sin_q).astype(BF16)
        qt_ref[0, r2:r3, :] = (x1 * sin_q + x2 * cos_q).astype(BF16)
        qt_ref[0, r3:r0 + HEAD_LANES, :] = qm[r3:r0 + HEAD_LANES, :].astype(BF16)


def _mla_proj(h1, pos, invf, gkv, gq, wdc, wdr, gckv, wuk, wuvt, wcq, wgate, gqn, wuqt, tm, scale):
    T, D = h1.shape
    full = lambda a: pl.BlockSpec(a.shape, lambda i: (0, 0))
    rows = lambda n: pl.BlockSpec((tm, n), lambda i: (i, 0))
    cols = lambda n: pl.BlockSpec((1, n, tm), lambda i: (i, 0, 0))
    hq = wuqt.shape[0]
    hv = wuvt.shape[0]
    hg = wgate.shape[1]
    weights = (invf, gkv, gq, wdc, wdr, gckv, wuk, wuvt, wcq, wgate, gqn, wuqt)
    return pl.pallas_call(
        functools.partial(_mla_proj_kernel, scale=scale),
        grid=(T // tm,),
        in_specs=[rows(D), pl.BlockSpec((1, tm), lambda i: (0, i))] + [full(w) for w in weights],
        out_specs=[cols(hq), rows(hq), cols(hv), rows(hg)],
        out_shape=[jax.ShapeDtypeStruct((T // tm, hq, tm), BF16), jax.ShapeDtypeStruct((T, hq), BF16),
                   jax.ShapeDtypeStruct((T // tm, hv, tm), BF16), jax.ShapeDtypeStruct((T, hg), BF16)],
        compiler_params=_params("arbitrary"),
        name="mla_proj",
    )(h1, pos.reshape(1, T), *weights)


ATTN_TILE = 1024
ATTN_UNIT_COLS = 256
ATTN_LAG_UNITS = 1


def _attn_kernel(qt_ref, k_ref, vt_ref, gate_ref, o_ref, *scratch, tm):
    tq = ATTN_TILE
    qc = ATTN_UNIT_COLS
    n_slices = tq // qc
    n_units = 2 * n_slices
    sub = tq // tm
    s_refs = scratch[:n_units]
    p_refs = scratch[n_units:2 * n_units]
    mx_ref, m_ref, alpha_ref, l_ref, acc_ref = scratch[2 * n_units:]
    qi = pl.program_id(2)
    lead = range(n_units - ATTN_LAG_UNITS)
    lag = range(n_units - ATTN_LAG_UNITS, n_units)

    m_ref[...] = jnp.full(m_ref.shape, -jnp.inf, F32)
    l_ref[...] = jnp.zeros(l_ref.shape, F32)
    acc_ref[...] = jnp.zeros(acc_ref.shape, F32)
    for u in lag:
        p_refs[u][...] = jnp.zeros(p_refs[u].shape, BF16)
    alpha_ref[...] = jnp.ones(alpha_ref.shape, F32)

    def head_slice(u):
        h, r = divmod(u, n_slices)
        return h, n_slices - 1 - r

    def scores(u, j):
        h, c = head_slice(u)
        hs = slice(h * HEAD_LANES, (h + 1) * HEAD_LANES)
        ks = pl.ds(pl.multiple_of(j * tq, tq), tq)
        t, off = divmod(c * qc, tm)
        s = _dot(k_ref[ks, hs], qt_ref[t, hs, off:off + qc])
        s_refs[u][...] = s
        mx_ref[u] = jnp.max(s, axis=0, keepdims=True)

    def diag_keys(u):
        return (head_slice(u)[1] + 1) * qc

    def probs(u, masked):
        c = head_slice(u)[1]
        nk = diag_keys(u) if masked else tq
        s = s_refs[u][0:nk, :]
        if masked:
            key = lax.broadcasted_iota(jnp.int32, (nk, qc), 0)
            qry = lax.broadcasted_iota(jnp.int32, (nk, qc), 1) + c * qc
            s = jnp.where(key <= qry, s, -jnp.inf)
            mx = jnp.max(s, axis=0, keepdims=True)
        else:
            mx = mx_ref[u]
        m_old = m_ref[u]
        m_new = jnp.maximum(m_old, mx)
        alpha_ref[u] = jnp.exp2(m_old - m_new)
        m_ref[u] = m_new
        p_refs[u][0:nk, :] = jnp.exp2(s - m_new).astype(BF16)

    def values(u, j, masked=False):
        h = u // n_slices
        vs = slice(h * V_HEAD, (h + 1) * V_HEAD)
        nk = diag_keys(u) if masked else tq
        pv = None
        for t in range(sub):
            n = min(nk - t * tm, tm)
            if n <= 0:
                break
            lhs = jnp.concatenate([vt_ref[j * sub + t, vs, 0:n], jnp.ones((BF16_SUBLANES, n), BF16)], axis=0)
            part = _dot(lhs, p_refs[u][t * tm:t * tm + n, :])
            pv = part if pv is None else pv + part
        alpha = alpha_ref[u]
        acc_ref[u] = alpha * acc_ref[u] + pv[:V_HEAD, :]
        l_ref[u] = alpha * l_ref[u] + pv[V_HEAD:V_HEAD + 1, :]

    for u in lead:
        scores(u, 0)

    def body(j, _):
        for u in lag:
            values(u, jnp.maximum(j - 1, 0))
        for u in lag:
            scores(u, j)
        for u in lead:
            probs(u, masked=False)
            values(u, j)
            scores(u, j + 1)
        for u in lag:
            probs(u, masked=False)
        return 0

    lax.fori_loop(0, qi, body, 0)
    for u in lag:
        values(u, jnp.maximum(qi - 1, 0))
    for u in lag:
        scores(u, qi)
    for u in range(n_units):
        probs(u, masked=True)
        values(u, qi, masked=True)

    o_t = jnp.concatenate(
        [jnp.concatenate([acc_ref[u] * (1.0 / l_ref[u]) for u in reversed(range(h * n_slices, (h + 1) * n_slices))],
                         axis=1) for h in range(2)], axis=0)
    o_ref[...] = (o_t.T * _silu(gate_ref[...].astype(F32))).astype(BF16)


def _mla_attn(qt, k, vt, gate, batch):
    T = k.shape[0]
    tm = qt.shape[2]
    tq = ATTN_TILE
    qc = ATTN_UNIT_COLS
    S = T // batch
    n_q = S // tq
    sub = tq // tm
    n_pairs = vt.shape[1] // LANES
    n_units = 2 * (tq // qc)
    stat = pltpu.VMEM((n_units, 1, qc), F32)
    return pl.pallas_call(
        functools.partial(_attn_kernel, tm=tm),
        grid=(batch, n_pairs, n_q),
        in_specs=[pl.BlockSpec((sub, 2 * HEAD_LANES, tm), lambda b, p, i: (b * n_q + i, p, 0)),
                  pl.BlockSpec((S, 2 * HEAD_LANES), lambda b, p, i: (b, p)),
                  pl.BlockSpec((S // tm, LANES, tm), lambda b, p, i: (b, p, 0)),
                  pl.BlockSpec((tq, LANES), lambda b, p, i: (b * n_q + i, p))],
        out_specs=pl.BlockSpec((tq, LANES), lambda b, p, i: (b * n_q + i, p)),
        out_shape=jax.ShapeDtypeStruct((T, vt.shape[1]), BF16),
        scratch_shapes=([pltpu.VMEM((tq, qc), F32)] * n_units + [pltpu.VMEM((tq, qc), BF16)] * n_units
                        + [stat, stat, stat, stat, pltpu.VMEM((n_units, V_HEAD, qc), F32)]),
        compiler_params=_params("arbitrary", "arbitrary", "arbitrary"),
        name="mla_attn",
    )(qt, k, vt, gate)


def _mla_out_kernel(o_ref, h_ref, w_ref, g_ref, out_ref):
    h = h_ref[...] + _dot(o_ref[...], w_ref[...])
    out_ref[...] = _rms_scale(h) * g_ref[...]


def _mla_out(o, h1, w, g, tm):
    T, D = h1.shape
    full = lambda a: pl.BlockSpec(a.shape, lambda i: (0, 0))
    rows = lambda n: pl.BlockSpec((tm, n), lambda i: (i, 0))
    return pl.pallas_call(
        _mla_out_kernel,
        grid=(T // tm,),
        in_specs=[rows(o.shape[1]), rows(D), full(w), full(g)],
        out_specs=rows(D),
        out_shape=jax.ShapeDtypeStruct((T, D), F32),
        compiler_params=_params("arbitrary"),
        name="mla_out",
    )(o, h1, w, g)


def _pad_cols(a, n):
    return jnp.pad(a, ((0, 0), (0, n - a.shape[1])))


def _head_slots(w_nope, w_rope):
    k, h = w_nope.shape[0], w_nope.shape[1]
    pad = jnp.zeros((k, h, HEAD_LANES - QK_NOPE - QK_ROPE), w_nope.dtype)
    return jnp.concatenate([w_nope, w_rope, pad], axis=-1).reshape(k, h * HEAD_LANES)


def _rotate_half_cols(w_rope):
    w1, w2 = jnp.split(w_rope, 2, axis=-1)
    return jnp.concatenate([-w2, w1], axis=-1)


def kernel(x, positions, g_pre, ssm_w_in, ssm_conv_w, ssm_conv_b, ssm_dt_bias, ssm_A_log, ssm_D, ssm_g_out,
           ssm_w_out, kv_g_in, kv_w_down, kv_g_latent, kv_w_up, mla_w_in, mla_g_q, mla_w_uq, mla_w_out, g_final):
    B, S, D = x.shape
    T = B * S
    d_inner = ssm_w_out.shape[1]
    n_heads = ssm_dt_bias.shape[1]
    conv_dim = ssm_conv_w.shape[2]
    kv_lora = kv_g_latent.shape[0]
    q_lora = mla_g_q.shape[1]
    H = MLA_HEADS
    assert d_inner == n_heads * SSM_HEAD_DIM and n_heads <= LANES
    assert S % SSD_CHUNK == 0 and ssm_conv_w.shape[1] - 1 <= CONV_HALO
    assert S % ATTN_TILE == 0 and ATTN_TILE % PROJ_ROWS == 0 and T % OUT_ROWS == 0
    assert mla_w_out.shape[1] == H * V_HEAD and H % 2 == 0
    x2d = x.reshape(T, D)

    w_in = jnp.pad(ssm_w_in[0].T, ((0, LANES - n_heads), (0, 0))).astype(BF16)
    z, xbc, dt_raw = _ssm_in_proj(x2d, g_pre[0][None, :], w_in, d_inner, conv_dim, tm=PROJ_ROWS)

    head_of_channel = jnp.arange(d_inner) // SSM_HEAD_DIM
    expand = (jnp.arange(LANES)[:, None] == head_of_channel[None, :]).astype(BF16)
    e2 = jnp.concatenate([expand, expand], axis=0)
    t_idx = jnp.arange(SSD_CHUNK)
    kw = ssm_conv_w.shape[1]
    shift = jnp.concatenate([(t_idx[:, None] - s == t_idx[None, :]) for s in range(1, kw)], axis=0).astype(BF16)
    h1 = _ssd_mixer(
        z, xbc, dt_raw, x2d, ssm_conv_w[0], ssm_conv_b[0][None, :],
        _pad_cols(ssm_dt_bias[0][None, :], LANES), _pad_cols(ssm_A_log[0][None, :], LANES),
        jnp.repeat(ssm_D[0], SSM_HEAD_DIM)[None, :], ssm_g_out[0][None, :], ssm_w_out[0].astype(BF16), e2, shift,
        batch=B)

    invf = (ROPE_BASE ** (-jnp.arange(0, QK_ROPE, 2, dtype=F32) / QK_ROPE))[:, None]
    wdc = kv_w_down[:, :kv_lora].astype(BF16)
    wr = kv_w_down[:, kv_lora:]
    slot1 = lambda w: jnp.pad(w, ((0, 0), (QK_NOPE, HEAD_LANES - QK_NOPE - QK_ROPE)))
    wdr = jnp.concatenate([slot1(wr), slot1(_rotate_half_cols(wr))], axis=1).astype(BF16)
    wup = kv_w_up.reshape(kv_lora, H, QK_NOPE + V_HEAD)
    wuk = _head_slots(wup[:, :, :QK_NOPE], jnp.zeros((kv_lora, H, QK_ROPE), F32)).astype(BF16)
    wuvt = wup[:, :, QK_NOPE:].reshape(kv_lora, H * V_HEAD).T.astype(BF16)
    w_in_b = mla_w_in[0]
    wcq = w_in_b[:, :q_lora].astype(BF16)
    wgate = w_in_b[:, q_lora:].astype(BF16)
    wq = mla_w_uq[0].reshape(q_lora, H, QK_NOPE + QK_ROPE)
    wq_nope, wq_rope = wq[:, :, :QK_NOPE], wq[:, :, QK_NOPE:]
    wuqt = _head_slots(wq_nope, wq_rope).T.astype(BF16)
    scale = float((QK_NOPE + QK_ROPE) ** -0.5 * math.log2(math.e))

    qt, k, vt, gate = _mla_proj(
        h1, positions, invf, kv_g_in[None, :], g_pre[1][None, :], wdc, wdr, kv_g_latent[None, :],
        wuk, wuvt, wcq, wgate, mla_g_q[0][None, :], wuqt, tm=PROJ_ROWS, scale=scale)
    o = _mla_attn(qt, k, vt, gate, batch=B)
    out = _mla_out(o, h1, mla_w_out[0].astype(BF16), g_final[None, :], tm=OUT_ROWS)
    return out.reshape(B, S, D)
```

```python
import functools
import math

import jax
import jax.numpy as jnp
from jax import lax
from jax.experimental import pallas as pl
from jax.experimental.pallas import tpu as pltpu

F32 = jnp.float32
BF16 = jnp.bfloat16

SSM_HEAD_DIM = 64
SSM_GROUPS = 4
SSM_STATE = 128
MLA_HEADS = 16
QK_NOPE = 64
QK_ROPE = 32
V_HEAD = 64
ROPE_BASE = 10000.0
EPS = 1e-6
LOG2_E = math.log2(math.e)

LANES = 128
BF16_SUBLANES = 16
HEAD_LANES = 128
SSD_CHUNK = 256
CONV_HALO = 8
PROJ_ROWS = 512
OUT_ROWS = 1024
VMEM_LIMIT = 56 * 1024 * 1024


def _params(*sem):
    return pltpu.CompilerParams(dimension_semantics=sem, vmem_limit_bytes=VMEM_LIMIT)


def _silu(x):
    return x * (1.0 / (1.0 + jnp.exp(-x)))


def _softplus(x):
    return jnp.maximum(x, 0.0) + jnp.log1p(jnp.exp(-jnp.abs(x)))


def _rms_scale(x):
    return x * lax.rsqrt(jnp.mean(x * x, axis=-1, keepdims=True) + EPS)


def _dot(a, b):
    return jnp.dot(a, b, preferred_element_type=F32)


def _nt_dot(a, b):
    return lax.dot_general(a, b, (((1,), (1,)), ((), ())), preferred_element_type=F32)


def _split2(x):
    hi = x.astype(BF16)
    lo = (x - hi.astype(F32)).astype(BF16)
    return jnp.concatenate([hi, lo], axis=-1)


def _ssm_in_proj_kernel(x_ref, g_ref, wt_ref, z_ref, xbc_ref, dt_ref):
    nz = z_ref.shape[1]
    nx = xbc_ref.shape[1]
    hn = (_rms_scale(x_ref[...]) * g_ref[...]).astype(BF16)
    z_ref[...] = _nt_dot(hn, wt_ref[:nz, :]).astype(BF16)
    xbc_ref[...] = _nt_dot(hn, wt_ref[nz:nz + nx, :]).astype(BF16)
    dt_ref[...] = _nt_dot(hn, wt_ref[nz + nx:, :])


def _ssm_in_proj(x2d, g, w, d_inner, conv_dim, tm):
    T, D = x2d.shape
    n_dt = w.shape[0] - d_inner - conv_dim
    full = lambda a: pl.BlockSpec(a.shape, lambda i: (0, 0))
    rows = lambda n: pl.BlockSpec((tm, n), lambda i: (i, 0))
    return pl.pallas_call(
        _ssm_in_proj_kernel,
        grid=(T // tm,),
        in_specs=[rows(D), full(g), full(w)],
        out_specs=[rows(d_inner), rows(conv_dim), rows(n_dt)],
        out_shape=[jax.ShapeDtypeStruct((T, d_inner), BF16),
                   jax.ShapeDtypeStruct((T, conv_dim), BF16),
                   jax.ShapeDtypeStruct((T, n_dt), F32)],
        compiler_params=_params("arbitrary"),
        name="ssm_in_proj",
    )(x2d, g, w)


def _ssd_kernel(z_ref, xbc_ref, dt_ref, x_ref, convw_ref, convb_ref, dtb_ref, alog_ref, dskip_ref,
                gout_ref, wout_ref, e2_ref, shift_ref, o_ref, hbuf, state, ybuf, *, d_inner, n_groups, d_state):
    Q = SSD_CHUNK
    gw = d_inner // n_groups
    heads_per_group = gw // SSM_HEAD_DIM
    pairs_per_group = gw // LANES
    kw = convw_ref.shape[0]

    @pl.when(pl.program_id(1) == 0)
    def _():
        hbuf[0:CONV_HALO, :] = jnp.zeros((CONV_HALO, hbuf.shape[1]), F32)
        state[...] = jnp.zeros(state.shape, F32)

    u_bf = xbc_ref[...]
    u = u_bf.astype(F32)
    shifted = _dot(shift_ref[...], u_bf)
    conv = convb_ref[...] + convw_ref[kw - 1:kw, :] * u
    for s in range(1, kw):
        conv = conv + convw_ref[kw - 1 - s:kw - s, :] * shifted[(s - 1) * Q:s * Q, :]
    hbuf[CONV_HALO:2 * CONV_HALO, :] = u[0:CONV_HALO, :]
    head = convb_ref[...]
    for k in range(kw):
        off = CONV_HALO - (kw - 1 - k)
        head = head + convw_ref[k:k + 1, :] * hbuf[off:off + CONV_HALO, :]
    hbuf[0:CONV_HALO, :] = u[Q - CONV_HALO:Q, :]
    conv = jnp.concatenate([head, conv[CONV_HALO:, :]], axis=0)
    xbc = _silu(conv)
    xs = xbc[:, :d_inner]

    dt = _softplus(dt_ref[...] + dtb_ref[...])
    dA = dt * (-jnp.exp(alog_ref[...]))
    row = lax.broadcasted_iota(jnp.int32, (Q, Q), 0)
    col = lax.broadcasted_iota(jnp.int32, (Q, Q), 1)
    causal = row >= col
    tril = causal.astype(BF16)
    hi = dA.astype(BF16)
    mid = (dA - hi.astype(F32)).astype(BF16)
    lo = (dA - hi.astype(F32) - mid.astype(F32)).astype(BF16)
    cum = (_dot(tril, hi) + _dot(tril, mid) + _dot(tril, lo)) * LOG2_E
    cum_t = cum.T
    cum_end = cum[Q - 1:Q, :]

    e2 = e2_ref[...]
    dt_x = _dot(_split2(dt), e2)
    ecum_x = _dot(_split2(jnp.exp2(cum)), e2)
    wend_x = _dot(_split2(jnp.exp2(cum_end - cum) * dt), e2)
    edec_x = _dot(_split2(jnp.broadcast_to(jnp.exp2(cum_end), (8, LANES))), e2)[0:1, :]

    lane = lax.broadcasted_iota(jnp.int32, (Q, LANES), 1)
    first_half = lane < SSM_HEAD_DIM
    xdt = xs * dt_x

    for g in range(n_groups):
        b_g = xbc[:, d_inner + g * d_state:d_inner + (g + 1) * d_state].astype(BF16)
        c_off = d_inner + n_groups * d_state
        c_g = xbc[:, c_off + g * d_state:c_off + (g + 1) * d_state].astype(BF16)
        cb = lax.dot_general(c_g, b_g, (((1,), (1,)), ((), ())), preferred_element_type=F32)
        gs = slice(g * gw, (g + 1) * gw)
        st = state[g]
        y_inter = _dot(c_g, st.astype(BF16)) * ecum_x[:, gs]
        xw = (xs[:, gs] * wend_x[:, gs]).astype(BF16)
        upd = lax.dot_general(b_g, xw, (((0,), (0,)), ((), ())), preferred_element_type=F32)
        state[g] = st * edec_x[:, gs] + upd
        for p in range(pairs_per_group):
            c0 = g * gw + p * LANES
            x_pair = xdt[:, c0:c0 + LANES]
            acc = y_inter[:, p * LANES:(p + 1) * LANES]
            for hh in range(2):
                h = g * heads_per_group + 2 * p + hh
                seg = cum[:, h:h + 1] - cum_t[h:h + 1, :]
                m = jnp.where(causal, cb * jnp.exp2(seg), 0.0).astype(BF16)
                keep = first_half if hh == 0 else jnp.logical_not(first_half)
                x_h = jnp.where(keep, x_pair, 0.0).astype(BF16)
                acc = acc + _dot(m, x_h)
            ybuf[:, c0:c0 + LANES] = acc

    y = (ybuf[...] + dskip_ref[...] * xs) * _silu(z_ref[...].astype(F32))
    for g in range(n_groups):
        gs = slice(g * gw, (g + 1) * gw)
        ybuf[:, gs] = _rms_scale(y[:, gs])
    yn = (ybuf[...] * gout_ref[...]).astype(BF16)
    o_ref[...] = x_ref[...] + _dot(yn, wout_ref[...])


def _ssd_mixer(z, xbc, dt_raw, x2d, convw, convb, dtb, alog, dskip, gout, wout, e2, shift, batch):
    T, d_inner = z.shape
    D = x2d.shape[1]
    conv_dim = xbc.shape[1]
    n_chunks = T // batch // SSD_CHUNK
    gw = d_inner // SSM_GROUPS
    full = lambda a: pl.BlockSpec(a.shape, lambda b, c: (0, 0))
    rows = lambda n: pl.BlockSpec((SSD_CHUNK, n), lambda b, c: (b * n_chunks + c, 0))
    kernel = functools.partial(_ssd_kernel, d_inner=d_inner, n_groups=SSM_GROUPS, d_state=SSM_STATE)
    return pl.pallas_call(
        kernel,
        grid=(batch, n_chunks),
        in_specs=[rows(d_inner), rows(conv_dim), rows(LANES), rows(D), full(convw), full(convb), full(dtb),
                  full(alog), full(dskip), full(gout), full(wout), full(e2), full(shift)],
        out_specs=rows(D),
        out_shape=jax.ShapeDtypeStruct((T, D), F32),
        scratch_shapes=[pltpu.VMEM((2 * CONV_HALO, conv_dim), F32),
                        pltpu.VMEM((SSM_GROUPS, SSM_STATE, gw), F32),
                        pltpu.VMEM((SSD_CHUNK, d_inner), F32)],
        compiler_params=_params("arbitrary", "arbitrary"),
        name="ssd_mixer",
    )(z, xbc, dt_raw, x2d, convw, convb, dtb, alog, dskip, gout, wout, e2, shift)


def _mla_proj_kernel(h_ref, pos_ref, invf_ref, gkv_ref, gq_ref, wdc_ref, wdr_ref,
                     gckv_ref, wuk_ref, wuvt_ref, wcq_ref, wgate_ref, gqn_ref, wuqt_ref,
                     qt_ref, k_ref, vt_ref, gate_ref, *, scale):
    tm = h_ref.shape[0]
    half = QK_ROPE // 2
    hs = _rms_scale(h_ref[...])
    hkv = (hs * gkv_ref[...]).astype(BF16)
    hq = (hs * gq_ref[...]).astype(BF16)

    ang = invf_ref[...] * pos_ref[...].astype(F32)
    cos = jnp.cos(ang)
    sin = jnp.sin(ang)

    ones = jnp.ones((QK_NOPE, tm), F32)
    tail = HEAD_LANES - QK_NOPE - QK_ROPE
    cos_k = jnp.concatenate([ones, cos, cos, ones[:tail]], axis=0).T
    sin_k = jnp.concatenate([0.0 * ones, sin, sin, 0.0 * ones[:tail]], axis=0).T
    ckv = (_rms_scale(_dot(hkv, wdc_ref[...])) * gckv_ref[...]).astype(BF16)
    kr2 = _dot(hkv, wdr_ref[...])
    k_rope = kr2[:, :HEAD_LANES] * cos_k + kr2[:, HEAD_LANES:] * sin_k
    vt_ref[0] = _nt_dot(wuvt_ref[...], ckv).astype(BF16)
    k_nope = _dot(ckv, wuk_ref[...])

    gate_ref[...] = _dot(hq, wgate_ref[...]).astype(BF16)
    cq = (_rms_scale(_dot(hq, wcq_ref[...])) * gqn_ref[...]).astype(BF16)
    qm = _nt_dot(wuqt_ref[...], cq)
    cos_q = cos * scale
    sin_q = sin * scale
    for h in range(k_ref.shape[1] // HEAD_LANES):
        r0 = h * HEAD_LANES
        r1 = r0 + QK_NOPE
        r2 = r1 + half
        r3 = r2 + half
        k_ref[:, r0:r0 + HEAD_LANES] = (k_nope[:, r0:r0 + HEAD_LANES] + k_rope).astype(BF16)
        x1 = qm[r1:r2, :]
        x2 = qm[r2:r3, :]
        qt_ref[0, r0:r1, :] = (qm[r0:r1, :] * scale).astype(BF16)
        qt_ref[0, r1:r2, :] = (x1 * cos_q - x2 * sin_q).astype(BF16)
        qt_ref[0, r2:r3, :] = (x1 * sin_q + x2 * cos_q).astype(BF16)
        qt_ref[0, r3:r0 + HEAD_LANES, :] = qm[r3:r0 + HEAD_LANES, :].astype(BF16)


def _mla_proj(h1, pos, invf, gkv, gq, wdc, wdr, gckv, wuk, wuvt, wcq, wgate, gqn, wuqt, tm, scale):
    T, D = h1.shape
    full = lambda a: pl.BlockSpec(a.shape, lambda i: (0, 0))
    rows = lambda n: pl.BlockSpec((tm, n), lambda i: (i, 0))
    cols = lambda n: pl.BlockSpec((1, n, tm), lambda i: (i, 0, 0))
    hq = wuqt.shape[0]
    hv = wuvt.shape[0]
    hg = wgate.shape[1]
    weights = (invf, gkv, gq, wdc, wdr, gckv, wuk, wuvt, wcq, wgate, gqn, wuqt)
    return pl.pallas_call(
        functools.partial(_mla_proj_kernel, scale=scale),
        grid=(T // tm,),
        in_specs=[rows(D), pl.BlockSpec((1, tm), lambda i: (0, i))] + [full(w) for w in weights],
        out_specs=[cols(hq), rows(hq), cols(hv), rows(hg)],
        out_shape=[jax.ShapeDtypeStruct((T // tm, hq, tm), BF16), jax.ShapeDtypeStruct((T, hq), BF16),
                   jax.ShapeDtypeStruct((T // tm, hv, tm), BF16), jax.ShapeDtypeStruct((T, hg), BF16)],
        compiler_params=_params("arbitrary"),
        name="mla_proj",
    )(h1, pos.reshape(1, T), *weights)


ATTN_TILE = 1024
ATTN_UNIT_COLS = 256


def _attn_kernel(qt_ref, k_ref, vt_ref, gate_ref, o_ref, *scratch, tm):
    tq = ATTN_TILE
    qc = ATTN_UNIT_COLS
    n_slices = tq // qc
    n_units = 2 * n_slices
    sub = tq // tm
    s_refs = scratch[:n_units]
    p_refs = scratch[n_units:2 * n_units]
    mx_ref, m_ref, alpha_ref, l_ref, acc_ref = scratch[2 * n_units:]
    qi = pl.program_id(2)

    m_ref[...] = jnp.full(m_ref.shape, -jnp.inf, F32)
    l_ref[...] = jnp.zeros(l_ref.shape, F32)
    acc_ref[...] = jnp.zeros(acc_ref.shape, F32)

    def head_slice(u):
        h, r = divmod(u, n_slices)
        return h, n_slices - 1 - r

    def scores(u, j):
        h, c = head_slice(u)
        hs = slice(h * HEAD_LANES, (h + 1) * HEAD_LANES)
        ks = pl.ds(pl.multiple_of(j * tq, tq), tq)
        t, off = divmod(c * qc, tm)
        s = _dot(k_ref[ks, hs], qt_ref[t, hs, off:off + qc])
        s_refs[u][...] = s
        mx_ref[u] = jnp.max(s, axis=0, keepdims=True)

    def diag_keys(u):
        return (head_slice(u)[1] + 1) * qc

    def probs(u, masked):
        c = head_slice(u)[1]
        nk = diag_keys(u) if masked else tq
        s = s_refs[u][0:nk, :]
        if masked:
            key = lax.broadcasted_iota(jnp.int32, (nk, qc), 0)
            qry = lax.broadcasted_iota(jnp.int32, (nk, qc), 1) + c * qc
            s = jnp.where(key <= qry, s, -jnp.inf)
            mx = jnp.max(s, axis=0, keepdims=True)
        else:
            mx = mx_ref[u]
        m_old = m_ref[u]
        m_new = jnp.maximum(m_old, mx)
        alpha_ref[u] = jnp.exp2(m_old - m_new)
        m_ref[u] = m_new
        p_refs[u][0:nk, :] = jnp.exp2(s - m_new).astype(BF16)

    def values(u, j, masked=False):
        h = u // n_slices
        vs = slice(h * V_HEAD, (h + 1) * V_HEAD)
        nk = diag_keys(u) if masked else tq
        pv = None
        for t in range(sub):
            n = min(nk - t * tm, tm)
            if n <= 0:
                break
            lhs = jnp.concatenate([vt_ref[j * sub + t, vs, 0:n], jnp.ones((BF16_SUBLANES, n), BF16)], axis=0)
            part = _dot(lhs, p_refs[u][t * tm:t * tm + n, :])
            pv = part if pv is None else pv + part
        alpha = alpha_ref[u]
        acc_ref[u] = alpha * acc_ref[u] + pv[:V_HEAD, :]
        l_ref[u] = alpha * l_ref[u] + pv[V_HEAD:V_HEAD + 1, :]

    for u in range(n_units):
        scores(u, 0)

    def body(j, _):
        for u in range(n_units):
            probs(u, masked=False)
            scores(u, j + 1)
            if u > 0:
                values(u - 1, j)
        values(n_units - 1, j)
        return 0

    lax.fori_loop(0, qi, body, 0)
    for u in range(n_units):
        probs(u, masked=True)
        if u > 0:
            values(u - 1, qi, masked=True)
    values(n_units - 1, qi, masked=True)

    o_t = jnp.concatenate(
        [jnp.concatenate([acc_ref[u] * (1.0 / l_ref[u]) for u in reversed(range(h * n_slices, (h + 1) * n_slices))],
                         axis=1) for h in range(2)], axis=0)
    o_ref[...] = (o_t.T * _silu(gate_ref[...].astype(F32))).astype(BF16)


def _mla_attn(qt, k, vt, gate, batch):
    T = k.shape[0]
    tm = qt.shape[2]
    tq = ATTN_TILE
    qc = ATTN_UNIT_COLS
    S = T // batch
    n_q = S // tq
    sub = tq // tm
    n_pairs = vt.shape[1] // LANES
    n_units = 2 * (tq // qc)
    stat = pltpu.VMEM((n_units, 1, qc), F32)
    return pl.pallas_call(
        functools.partial(_attn_kernel, tm=tm),
        grid=(batch, n_pairs, n_q),
        in_specs=[pl.BlockSpec((sub, 2 * HEAD_LANES, tm), lambda b, p, i: (b * n_q + i, p, 0)),
                  pl.BlockSpec((S, 2 * HEAD_LANES), lambda b, p, i: (b, p)),
                  pl.BlockSpec((S // tm, LANES, tm), lambda b, p, i: (b, p, 0)),
                  pl.BlockSpec((tq, LANES), lambda b, p, i: (b * n_q + i, p))],
        out_specs=pl.BlockSpec((tq, LANES), lambda b, p, i: (b * n_q + i, p)),
        out_shape=jax.ShapeDtypeStruct((T, vt.shape[1]), BF16),
        scratch_shapes=([pltpu.VMEM((tq, qc), F32)] * n_units + [pltpu.VMEM((tq, qc), BF16)] * n_units
                        + [stat, stat, stat, stat, pltpu.VMEM((n_units, V_HEAD, qc), F32)]),
        compiler_params=_params("arbitrary", "arbitrary", "arbitrary"),
        name="mla_attn",
    )(qt, k, vt, gate)


def _mla_out_kernel(o_ref, h_ref, w_ref, g_ref, out_ref):
    h = h_ref[...] + _dot(o_ref[...], w_ref[...])
    out_ref[...] = _rms_scale(h) * g_ref[...]


def _mla_out(o, h1, w, g, tm):
    T, D = h1.shape
    full = lambda a: pl.BlockSpec(a.shape, lambda i: (0, 0))
    rows = lambda n: pl.BlockSpec((tm, n), lambda i: (i, 0))
    return pl.pallas_call(
        _mla_out_kernel,
        grid=(T // tm,),
        in_specs=[rows(o.shape[1]), rows(D), full(w), full(g)],
        out_specs=rows(D),
        out_shape=jax.ShapeDtypeStruct((T, D), F32),
        compiler_params=_params("arbitrary"),
        name="mla_out",
    )(o, h1, w, g)


def _pad_cols(a, n):
    return jnp.pad(a, ((0, 0), (0, n - a.shape[1])))


def _head_slots(w_nope, w_rope):
    k, h = w_nope.shape[0], w_nope.shape[1]
    pad = jnp.zeros((k, h, HEAD_LANES - QK_NOPE - QK_ROPE), w_nope.dtype)
    return jnp.concatenate([w_nope, w_rope, pad], axis=-1).reshape(k, h * HEAD_LANES)


def _rotate_half_cols(w_rope):
    w1, w2 = jnp.split(w_rope, 2, axis=-1)
    return jnp.concatenate([-w2, w1], axis=-1)


def kernel(x, positions, g_pre, ssm_w_in, ssm_conv_w, ssm_conv_b, ssm_dt_bias, ssm_A_log, ssm_D, ssm_g_out,
           ssm_w_out, kv_g_in, kv_w_down, kv_g_latent, kv_w_up, mla_w_in, mla_g_q, mla_w_uq, mla_w_out, g_final):
    B, S, D = x.shape
    T = B * S
    d_inner = ssm_w_out.shape[1]
    n_heads = ssm_dt_bias.shape[1]
    conv_dim = ssm_conv_w.shape[2]
    kv_lora = kv_g_latent.shape[0]
    q_lora = mla_g_q.shape[1]
    H = MLA_HEADS
    assert d_inner == n_heads * SSM_HEAD_DIM and n_heads <= LANES
    assert S % SSD_CHUNK == 0 and ssm_conv_w.shape[1] - 1 <= CONV_HALO
    assert S % ATTN_TILE == 0 and ATTN_TILE % PROJ_ROWS == 0 and T % OUT_ROWS == 0
    assert mla_w_out.shape[1] == H * V_HEAD and H % 2 == 0
    x2d = x.reshape(T, D)

    w_in = jnp.pad(ssm_w_in[0].T, ((0, LANES - n_heads), (0, 0))).astype(BF16)
    z, xbc, dt_raw = _ssm_in_proj(x2d, g_pre[0][None, :], w_in, d_inner, conv_dim, tm=PROJ_ROWS)

    head_of_channel = jnp.arange(d_inner) // SSM_HEAD_DIM
    expand = (jnp.arange(LANES)[:, None] == head_of_channel[None, :]).astype(BF16)
    e2 = jnp.concatenate([expand, expand], axis=0)
    t_idx = jnp.arange(SSD_CHUNK)
    kw = ssm_conv_w.shape[1]
    shift = jnp.concatenate([(t_idx[:, None] - s == t_idx[None, :]) for s in range(1, kw)], axis=0).astype(BF16)
    h1 = _ssd_mixer(
        z, xbc, dt_raw, x2d, ssm_conv_w[0], ssm_conv_b[0][None, :],
        _pad_cols(ssm_dt_bias[0][None, :], LANES), _pad_cols(ssm_A_log[0][None, :], LANES),
        jnp.repeat(ssm_D[0], SSM_HEAD_DIM)[None, :], ssm_g_out[0][None, :], ssm_w_out[0].astype(BF16), e2, shift,
        batch=B)

    invf = (ROPE_BASE ** (-jnp.arange(0, QK_ROPE, 2, dtype=F32) / QK_ROPE))[:, None]
    wdc = kv_w_down[:, :kv_lora].astype(BF16)
    wr = kv_w_down[:, kv_lora:]
    slot1 = lambda w: jnp.pad(w, ((0, 0), (QK_NOPE, HEAD_LANES - QK_NOPE - QK_ROPE)))
    wdr = jnp.concatenate([slot1(wr), slot1(_rotate_half_cols(wr))], axis=1).astype(BF16)
    wup = kv_w_up.reshape(kv_lora, H, QK_NOPE + V_HEAD)
    wuk = _head_slots(wup[:, :, :QK_NOPE], jnp.zeros((kv_lora, H, QK_ROPE), F32)).astype(BF16)
    wuvt = wup[:, :, QK_NOPE:].reshape(kv_lora, H * V_HEAD).T.astype(BF16)
    w_in_b = mla_w_in[0]
    wcq = w_in_b[:, :q_lora].astype(BF16)
    wgate = w_in_b[:, q_lora:].astype(BF16)
    wq = mla_w_uq[0].reshape(q_lora, H, QK_NOPE + QK_ROPE)
    wq_nope, wq_rope = wq[:, :, :QK_NOPE], wq[:, :, QK_NOPE:]
    wuqt = _head_slots(wq_nope, wq_rope).T.astype(BF16)
    scale = float((QK_NOPE + QK_ROPE) ** -0.5 * math.log2(math.e))

    qt, k, vt, gate = _mla_proj(
        h1, positions, invf, kv_g_in[None, :], g_pre[1][None, :], wdc, wdr, kv_g_latent[None, :],
        wuk, wuvt, wcq, wgate, mla_g_q[0][None, :], wuqt, tm=PROJ_ROWS, scale=scale)
    o = _mla_attn(qt, k, vt, gate, batch=B)
    out = _mla_out(o, h1, mla_w_out[0].astype(BF16), g_final[None, :], tm=OUT_ROWS)
    return out.reshape(B, S, D)
```

```python
import functools
import math

import jax
import jax.numpy as jnp
from jax import lax
from jax.experimental import pallas as pl
from jax.experimental.pallas import tpu as pltpu

F32 = jnp.float32
BF16 = jnp.bfloat16

SSM_HEAD_DIM = 64
SSM_GROUPS = 4
SSM_STATE = 128
MLA_HEADS = 16
QK_NOPE = 64
QK_ROPE = 32
V_HEAD = 64
ROPE_BASE = 10000.0
EPS = 1e-6
LOG2_E = math.log2(math.e)

LANES = 128
BF16_SUBLANES = 16
HEAD_LANES = 128
SSD_CHUNK = 256
CONV_HALO = 8
PROJ_ROWS = 512
OUT_ROWS = 1024
VMEM_LIMIT = 56 * 1024 * 1024


def _params(*sem):
    return pltpu.CompilerParams(dimension_semantics=sem, vmem_limit_bytes=VMEM_LIMIT)


def _silu(x):
    return x * (1.0 / (1.0 + jnp.exp(-x)))


def _softplus(x):
    return jnp.maximum(x, 0.0) + jnp.log1p(jnp.exp(-jnp.abs(x)))


def _rms_scale(x):
    return x * lax.rsqrt(jnp.mean(x * x, axis=-1, keepdims=True) + EPS)


def _dot(a, b):
    return jnp.dot(a, b, preferred_element_type=F32)


def _nt_dot(a, b):
    return lax.dot_general(a, b, (((1,), (1,)), ((), ())), preferred_element_type=F32)


def _split2(x):
    hi = x.astype(BF16)
    lo = (x - hi.astype(F32)).astype(BF16)
    return jnp.concatenate([hi, lo], axis=-1)


def _ssm_in_proj_kernel(x_ref, g_ref, wt_ref, z_ref, xbc_ref, dt_ref):
    nz = z_ref.shape[1]
    nx = xbc_ref.shape[1]
    hn = (_rms_scale(x_ref[...]) * g_ref[...]).astype(BF16)
    z_ref[...] = _nt_dot(hn, wt_ref[:nz, :]).astype(BF16)
    xbc_ref[...] = _nt_dot(hn, wt_ref[nz:nz + nx, :]).astype(BF16)
    dt_ref[...] = _nt_dot(hn, wt_ref[nz + nx:, :])


def _ssm_in_proj(x2d, g, w, d_inner, conv_dim, tm):
    T, D = x2d.shape
    n_dt = w.shape[0] - d_inner - conv_dim
    full = lambda a: pl.BlockSpec(a.shape, lambda i: (0, 0))
    rows = lambda n: pl.BlockSpec((tm, n), lambda i: (i, 0))
    return pl.pallas_call(
        _ssm_in_proj_kernel,
        grid=(T // tm,),
        in_specs=[rows(D), full(g), full(w)],
        out_specs=[rows(d_inner), rows(conv_dim), rows(n_dt)],
        out_shape=[jax.ShapeDtypeStruct((T, d_inner), BF16),
                   jax.ShapeDtypeStruct((T, conv_dim), BF16),
                   jax.ShapeDtypeStruct((T, n_dt), F32)],
        compiler_params=_params("arbitrary"),
        name="ssm_in_proj",
    )(x2d, g, w)


def _ssd_kernel(z_ref, xbc_ref, dt_ref, x_ref, convw_ref, convb_ref, dtb_ref, alog_ref, dskip_ref,
                gout_ref, wout_ref, e2_ref, shift_ref, o_ref, hbuf, state, ybuf, *, d_inner, n_groups, d_state):
    Q = SSD_CHUNK
    gw = d_inner // n_groups
    heads_per_group = gw // SSM_HEAD_DIM
    pairs_per_group = gw // LANES
    kw = convw_ref.shape[0]

    @pl.when(pl.program_id(1) == 0)
    def _():
        hbuf[0:CONV_HALO, :] = jnp.zeros((CONV_HALO, hbuf.shape[1]), F32)
        state[...] = jnp.zeros(state.shape, F32)

    u_bf = xbc_ref[...]
    u = u_bf.astype(F32)
    shifted = _dot(shift_ref[...], u_bf)
    conv = convb_ref[...] + convw_ref[kw - 1:kw, :] * u
    for s in range(1, kw):
        conv = conv + convw_ref[kw - 1 - s:kw - s, :] * shifted[(s - 1) * Q:s * Q, :]
    hbuf[CONV_HALO:2 * CONV_HALO, :] = u[0:CONV_HALO, :]
    head = convb_ref[...]
    for k in range(kw):
        off = CONV_HALO - (kw - 1 - k)
        head = head + convw_ref[k:k + 1, :] * hbuf[off:off + CONV_HALO, :]
    hbuf[0:CONV_HALO, :] = u[Q - CONV_HALO:Q, :]
    conv = jnp.concatenate([head, conv[CONV_HALO:, :]], axis=0)
    xbc = _silu(conv)
    xs = xbc[:, :d_inner]

    dt = _softplus(dt_ref[...] + dtb_ref[...])
    dA = dt * (-jnp.exp(alog_ref[...]))
    row = lax.broadcasted_iota(jnp.int32, (Q, Q), 0)
    col = lax.broadcasted_iota(jnp.int32, (Q, Q), 1)
    causal = row >= col
    tril = causal.astype(BF16)
    hi = dA.astype(BF16)
    mid = (dA - hi.astype(F32)).astype(BF16)
    lo = (dA - hi.astype(F32) - mid.astype(F32)).astype(BF16)
    cum = (_dot(tril, hi) + _dot(tril, mid) + _dot(tril, lo)) * LOG2_E
    cum_t = cum.T
    cum_end = cum[Q - 1:Q, :]

    e2 = e2_ref[...]
    dt_x = _dot(_split2(dt), e2)
    ecum_x = _dot(_split2(jnp.exp2(cum)), e2)
    wend_x = _dot(_split2(jnp.exp2(cum_end - cum) * dt), e2)
    edec_x = _dot(_split2(jnp.broadcast_to(jnp.exp2(cum_end), (8, LANES))), e2)[0:1, :]

    lane = lax.broadcasted_iota(jnp.int32, (Q, LANES), 1)
    first_half = lane < SSM_HEAD_DIM
    xdt = xs * dt_x

    for g in range(n_groups):
        b_g = xbc[:, d_inner + g * d_state:d_inner + (g + 1) * d_state].astype(BF16)
        c_off = d_inner + n_groups * d_state
        c_g = xbc[:, c_off + g * d_state:c_off + (g + 1) * d_state].astype(BF16)
        cb = lax.dot_general(c_g, b_g, (((1,), (1,)), ((), ())), preferred_element_type=F32)
        gs = slice(g * gw, (g + 1) * gw)
        st = state[g]
        y_inter = _dot(c_g, st.astype(BF16)) * ecum_x[:, gs]
        xw = (xs[:, gs] * wend_x[:, gs]).astype(BF16)
        upd = lax.dot_general(b_g, xw, (((0,), (0,)), ((), ())), preferred_element_type=F32)
        state[g] = st * edec_x[:, gs] + upd
        for p in range(pairs_per_group):
            c0 = g * gw + p * LANES
            x_pair = xdt[:, c0:c0 + LANES]
            acc = y_inter[:, p * LANES:(p + 1) * LANES]
            for hh in range(2):
                h = g * heads_per_group + 2 * p + hh
                seg = cum[:, h:h + 1] - cum_t[h:h + 1, :]
                m = jnp.where(causal, cb * jnp.exp2(seg), 0.0).astype(BF16)
                keep = first_half if hh == 0 else jnp.logical_not(first_half)
                x_h = jnp.where(keep, x_pair, 0.0).astype(BF16)
                acc = acc + _dot(m, x_h)
            ybuf[:, c0:c0 + LANES] = acc

    y = (ybuf[...] + dskip_ref[...] * xs) * _silu(z_ref[...].astype(F32))
    for g in range(n_groups):
        gs = slice(g * gw, (g + 1) * gw)
        ybuf[:, gs] = _rms_scale(y[:, gs])
    yn = (ybuf[...] * gout_ref[...]).astype(BF16)
    o_ref[...] = x_ref[...] + _dot(yn, wout_ref[...])


def _ssd_mixer(z, xbc, dt_raw, x2d, convw, convb, dtb, alog, dskip, gout, wout, e2, shift, batch):
    T, d_inner = z.shape
    D = x2d.shape[1]
    conv_dim = xbc.shape[1]
    n_chunks = T // batch // SSD_CHUNK
    gw = d_inner // SSM_GROUPS
    full = lambda a: pl.BlockSpec(a.shape, lambda b, c: (0, 0))
    rows = lambda n: pl.BlockSpec((SSD_CHUNK, n), lambda b, c: (b * n_chunks + c, 0))
    kernel = functools.partial(_ssd_kernel, d_inner=d_inner, n_groups=SSM_GROUPS, d_state=SSM_STATE)
    return pl.pallas_call(
        kernel,
        grid=(batch, n_chunks),
        in_specs=[rows(d_inner), rows(conv_dim), rows(LANES), rows(D), full(convw), full(convb), full(dtb),
                  full(alog), full(dskip), full(gout), full(wout), full(e2), full(shift)],
        out_specs=rows(D),
        out_shape=jax.ShapeDtypeStruct((T, D), F32),
        scratch_shapes=[pltpu.VMEM((2 * CONV_HALO, conv_dim), F32),
                        pltpu.VMEM((SSM_GROUPS, SSM_STATE, gw), F32),
                        pltpu.VMEM((SSD_CHUNK, d_inner), F32)],
        compiler_params=_params("arbitrary", "arbitrary"),
        name="ssd_mixer",
    )(z, xbc, dt_raw, x2d, convw, convb, dtb, alog, dskip, gout, wout, e2, shift)


def _mla_proj_kernel(h_ref, pos_ref, invf_ref, gkv_ref, gq_ref, wdc_ref, wdr_ref,
                     gckv_ref, wuk_ref, wuvt_ref, wcq_ref, wgate_ref, gqn_ref, wuqt_ref,
                     qt_ref, k_ref, vt_ref, gate_ref, *, scale):
    tm = h_ref.shape[0]
    half = QK_ROPE // 2
    hs = _rms_scale(h_ref[...])
    hkv = (hs * gkv_ref[...]).astype(BF16)
    hq = (hs * gq_ref[...]).astype(BF16)
    gate_ref[...] = _dot(hq, wgate_ref[...]).astype(BF16)
    cq = (_rms_scale(_dot(hq, wcq_ref[...])) * gqn_ref[...]).astype(BF16)
    qm = _nt_dot(wuqt_ref[...], cq)

    ckv = (_rms_scale(_dot(hkv, wdc_ref[...])) * gckv_ref[...]).astype(BF16)
    kr2 = _dot(hkv, wdr_ref[...])
    vt_ref[0] = _nt_dot(wuvt_ref[...], ckv).astype(BF16)
    k_nope = _dot(ckv, wuk_ref[...])

    ang = invf_ref[...] * pos_ref[...].astype(F32)
    cos = jnp.cos(ang)
    sin = jnp.sin(ang)

    ones = jnp.ones((QK_NOPE, tm), F32)
    tail = HEAD_LANES - QK_NOPE - QK_ROPE
    cos_k = jnp.concatenate([ones, cos, cos, ones[:tail]], axis=0).T
    sin_k = jnp.concatenate([0.0 * ones, sin, sin, 0.0 * ones[:tail]], axis=0).T
    k_rope = kr2[:, :HEAD_LANES] * cos_k + kr2[:, HEAD_LANES:] * sin_k

    cos_q = cos * scale
    sin_q = sin * scale
    for h in range(k_ref.shape[1] // HEAD_LANES):
        r0 = h * HEAD_LANES
        r1 = r0 + QK_NOPE
        r2 = r1 + half
        r3 = r2 + half
        k_ref[:, r0:r0 + HEAD_LANES] = (k_nope[:, r0:r0 + HEAD_LANES] + k_rope).astype(BF16)
        x1 = qm[r1:r2, :]
        x2 = qm[r2:r3, :]
        qt_ref[0, r0:r1, :] = (qm[r0:r1, :] * scale).astype(BF16)
        qt_ref[0, r1:r2, :] = (x1 * cos_q - x2 * sin_q).astype(BF16)
        qt_ref[0, r2:r3, :] = (x1 * sin_q + x2 * cos_q).astype(BF16)
        qt_ref[0, r3:r0 + HEAD_LANES, :] = qm[r3:r0 + HEAD_LANES, :].astype(BF16)


def _mla_proj(h1, pos, invf, gkv, gq, wdc, wdr, gckv, wuk, wuvt, wcq, wgate, gqn, wuqt, tm, scale):
    T, D = h1.shape
    full = lambda a: pl.BlockSpec(a.shape, lambda i: (0, 0))
    rows = lambda n: pl.BlockSpec((tm, n), lambda i: (i, 0))
    cols = lambda n: pl.BlockSpec((1, n, tm), lambda i: (i, 0, 0))
    hq = wuqt.shape[0]
    hv = wuvt.shape[0]
    hg = wgate.shape[1]
    weights = (invf, gkv, gq, wdc, wdr, gckv, wuk, wuvt, wcq, wgate, gqn, wuqt)
    return pl.pallas_call(
        functools.partial(_mla_proj_kernel, scale=scale),
        grid=(T // tm,),
        in_specs=[rows(D), pl.BlockSpec((1, tm), lambda i: (0, i))] + [full(w) for w in weights],
        out_specs=[cols(hq), rows(hq), cols(hv), rows(hg)],
        out_shape=[jax.ShapeDtypeStruct((T // tm, hq, tm), BF16), jax.ShapeDtypeStruct((T, hq), BF16),
                   jax.ShapeDtypeStruct((T // tm, hv, tm), BF16), jax.ShapeDtypeStruct((T, hg), BF16)],
        compiler_params=_params("arbitrary"),
        name="mla_proj",
    )(h1, pos.reshape(1, T), *weights)


ATTN_TILE = 1024
ATTN_UNIT_COLS = 256


def _attn_kernel(qt_ref, k_ref, vt_ref, gate_ref, o_ref, *scratch, tm):
    tq = ATTN_TILE
    qc = ATTN_UNIT_COLS
    n_slices = tq // qc
    n_units = 2 * n_slices
    sub = tq // tm
    s_refs = scratch[:n_units]
    p_refs = scratch[n_units:2 * n_units]
    mx_ref, m_ref, alpha_ref, l_ref, acc_ref = scratch[2 * n_units:]
    qi = pl.program_id(2)

    m_ref[...] = jnp.full(m_ref.shape, -jnp.inf, F32)
    l_ref[...] = jnp.zeros(l_ref.shape, F32)
    acc_ref[...] = jnp.zeros(acc_ref.shape, F32)

    def head_slice(u):
        h, r = divmod(u, n_slices)
        return h, n_slices - 1 - r

    def scores(u, j):
        h, c = head_slice(u)
        hs = slice(h * HEAD_LANES, (h + 1) * HEAD_LANES)
        ks = pl.ds(pl.multiple_of(j * tq, tq), tq)
        t, off = divmod(c * qc, tm)
        s = _dot(k_ref[ks, hs], qt_ref[t, hs, off:off + qc])
        s_refs[u][...] = s
        mx_ref[u] = jnp.max(s, axis=0, keepdims=True)

    def diag_keys(u):
        return (head_slice(u)[1] + 1) * qc

    def probs(u, masked):
        c = head_slice(u)[1]
        nk = diag_keys(u) if masked else tq
        s = s_refs[u][0:nk, :]
        if masked:
            key = lax.broadcasted_iota(jnp.int32, (nk, qc), 0)
            qry = lax.broadcasted_iota(jnp.int32, (nk, qc), 1) + c * qc
            s = jnp.where(key <= qry, s, -jnp.inf)
            mx = jnp.max(s, axis=0, keepdims=True)
        else:
            mx = mx_ref[u]
        m_old = m_ref[u]
        m_new = jnp.maximum(m_old, mx)
        alpha_ref[u] = jnp.exp2(m_old - m_new)
        m_ref[u] = m_new
        p_refs[u][0:nk, :] = jnp.exp2(s - m_new).astype(BF16)

    def values(u, j, masked=False):
        h = u // n_slices
        vs = slice(h * V_HEAD, (h + 1) * V_HEAD)
        nk = diag_keys(u) if masked else tq
        pv = None
        for t in range(sub):
            n = min(nk - t * tm, tm)
            if n <= 0:
                break
            lhs = jnp.concatenate([vt_ref[j * sub + t, vs, 0:n], jnp.ones((BF16_SUBLANES, n), BF16)], axis=0)
            part = _dot(lhs, p_refs[u][t * tm:t * tm + n, :])
            pv = part if pv is None else pv + part
        alpha = alpha_ref[u]
        acc_ref[u] = alpha * acc_ref[u] + pv[:V_HEAD, :]
        l_ref[u] = alpha * l_ref[u] + pv[V_HEAD:V_HEAD + 1, :]

    for u in range(n_units):
        scores(u, 0)

    def body(j, _):
        for u in range(n_units):
            probs(u, masked=False)
            scores(u, j + 1)
            if u > 0:
                values(u - 1, j)
        values(n_units - 1, j)
        return 0

    lax.fori_loop(0, qi, body, 0)
    for u in range(n_units):
        probs(u, masked=True)
        if u > 0:
            values(u - 1, qi, masked=True)
    values(n_units - 1, qi, masked=True)

    o_t = jnp.concatenate(
        [jnp.concatenate([acc_ref[u] * (1.0 / l_ref[u]) for u in reversed(range(h * n_slices, (h + 1) * n_slices))],
                         axis=1) for h in range(2)], axis=0)
    o_ref[...] = (o_t.T * _silu(gate_ref[...].astype(F32))).astype(BF16)


def _mla_attn(qt, k, vt, gate, batch):
    T = k.shape[0]
    tm = qt.shape[2]
    tq = ATTN_TILE
    qc = ATTN_UNIT_COLS
    S = T // batch
    n_q = S // tq
    sub = tq // tm
    n_pairs = vt.shape[1] // LANES
    n_units = 2 * (tq // qc)
    stat = pltpu.VMEM((n_units, 1, qc), F32)
    return pl.pallas_call(
        functools.partial(_attn_kernel, tm=tm),
        grid=(batch, n_pairs, n_q),
        in_specs=[pl.BlockSpec((sub, 2 * HEAD_LANES, tm), lambda b, p, i: (b * n_q + i, p, 0)),
                  pl.BlockSpec((S, 2 * HEAD_LANES), lambda b, p, i: (b, p)),
                  pl.BlockSpec((S // tm, LANES, tm), lambda b, p, i: (b, p, 0)),
                  pl.BlockSpec((tq, LANES), lambda b, p, i: (b * n_q + i, p))],
        out_specs=pl.BlockSpec((tq, LANES), lambda b, p, i: (b * n_q + i, p)),
        out_shape=jax.ShapeDtypeStruct((T, vt.shape[1]), BF16),
        scratch_shapes=([pltpu.VMEM((tq, qc), F32)] * n_units + [pltpu.VMEM((tq, qc), BF16)] * n_units
                        + [stat, stat, stat, stat, pltpu.VMEM((n_units, V_HEAD, qc), F32)]),
        compiler_params=_params("arbitrary", "arbitrary", "arbitrary"),
        name="mla_attn",
    )(qt, k, vt, gate)


def _mla_out_kernel(o_ref, h_ref, w_ref, g_ref, out_ref):
    h = h_ref[...] + _dot(o_ref[...], w_ref[...])
    out_ref[...] = _rms_scale(h) * g_ref[...]


def _mla_out(o, h1, w, g, tm):
    T, D = h1.shape
    full = lambda a: pl.BlockSpec(a.shape, lambda i: (0, 0))
    rows = lambda n: pl.BlockSpec((tm, n), lambda i: (i, 0))
    return pl.pallas_call(
        _mla_out_kernel,
        grid=(T // tm,),
        in_specs=[rows(o.shape[1]), rows(D), full(w), full(g)],
        out_specs=rows(D),
        out_shape=jax.ShapeDtypeStruct((T, D), F32),
        compiler_params=_params("arbitrary"),
        name="mla_out",
    )(o, h1, w, g)


def _pad_cols(a, n):
    return jnp.pad(a, ((0, 0), (0, n - a.shape[1])))


def _head_slots(w_nope, w_rope):
    k, h = w_nope.shape[0], w_nope.shape[1]
    pad = jnp.zeros((k, h, HEAD_LANES - QK_NOPE - QK_ROPE), w_nope.dtype)
    return jnp.concatenate([w_nope, w_rope, pad], axis=-1).reshape(k, h * HEAD_LANES)


def _rotate_half_cols(w_rope):
    w1, w2 = jnp.split(w_rope, 2, axis=-1)
    return jnp.concatenate([-w2, w1], axis=-1)


def kernel(x, positions, g_pre, ssm_w_in, ssm_conv_w, ssm_conv_b, ssm_dt_bias, ssm_A_log, ssm_D, ssm_g_out,
           ssm_w_out, kv_g_in, kv_w_down, kv_g_latent, kv_w_up, mla_w_in, mla_g_q, mla_w_uq, mla_w_out, g_final):
    B, S, D = x.shape
    T = B * S
    d_inner = ssm_w_out.shape[1]
    n_heads = ssm_dt_bias.shape[1]
    conv_dim = ssm_conv_w.shape[2]
    kv_lora = kv_g_latent.shape[0]
    q_lora = mla_g_q.shape[1]
    H = MLA_HEADS
    assert d_inner == n_heads * SSM_HEAD_DIM and n_heads <= LANES
    assert S % SSD_CHUNK == 0 and ssm_conv_w.shape[1] - 1 <= CONV_HALO
    assert S % ATTN_TILE == 0 and ATTN_TILE % PROJ_ROWS == 0 and T % OUT_ROWS == 0
    assert mla_w_out.shape[1] == H * V_HEAD and H % 2 == 0
    x2d = x.reshape(T, D)

    w_in = jnp.pad(ssm_w_in[0].T, ((0, LANES - n_heads), (0, 0))).astype(BF16)
    z, xbc, dt_raw = _ssm_in_proj(x2d, g_pre[0][None, :], w_in, d_inner, conv_dim, tm=PROJ_ROWS)

    head_of_channel = jnp.arange(d_inner) // SSM_HEAD_DIM
    expand = (jnp.arange(LANES)[:, None] == head_of_channel[None, :]).astype(BF16)
    e2 = jnp.concatenate([expand, expand], axis=0)
    t_idx = jnp.arange(SSD_CHUNK)
    kw = ssm_conv_w.shape[1]
    shift = jnp.concatenate([(t_idx[:, None] - s == t_idx[None, :]) for s in range(1, kw)], axis=0).astype(BF16)
    h1 = _ssd_mixer(
        z, xbc, dt_raw, x2d, ssm_conv_w[0], ssm_conv_b[0][None, :],
        _pad_cols(ssm_dt_bias[0][None, :], LANES), _pad_cols(ssm_A_log[0][None, :], LANES),
        jnp.repeat(ssm_D[0], SSM_HEAD_DIM)[None, :], ssm_g_out[0][None, :], ssm_w_out[0].astype(BF16), e2, shift,
        batch=B)

    invf = (ROPE_BASE ** (-jnp.arange(0, QK_ROPE, 2, dtype=F32) / QK_ROPE))[:, None]
    wdc = kv_w_down[:, :kv_lora].astype(BF16)
    wr = kv_w_down[:, kv_lora:]
    slot1 = lambda w: jnp.pad(w, ((0, 0), (QK_NOPE, HEAD_LANES - QK_NOPE - QK_ROPE)))
    wdr = jnp.concatenate([slot1(wr), slot1(_rotate_half_cols(wr))], axis=1).astype(BF16)
    wup = kv_w_up.reshape(kv_lora, H, QK_NOPE + V_HEAD)
    wuk = _head_slots(wup[:, :, :QK_NOPE], jnp.zeros((kv_lora, H, QK_ROPE), F32)).astype(BF16)
    wuvt = wup[:, :, QK_NOPE:].reshape(kv_lora, H * V_HEAD).T.astype(BF16)
    w_in_b = mla_w_in[0]
    wcq = w_in_b[:, :q_lora].astype(BF16)
    wgate = w_in_b[:, q_lora:].astype(BF16)
    wq = mla_w_uq[0].reshape(q_lora, H, QK_NOPE + QK_ROPE)
    wq_nope, wq_rope = wq[:, :, :QK_NOPE], wq[:, :, QK_NOPE:]
    wuqt = _head_slots(wq_nope, wq_rope).T.astype(BF16)
    scale = float((QK_NOPE + QK_ROPE) ** -0.5 * math.log2(math.e))

    qt, k, vt, gate = _mla_proj(
        h1, positions, invf, kv_g_in[None, :], g_pre[1][None, :], wdc, wdr, kv_g_latent[None, :],
        wuk, wuvt, wcq, wgate, mla_g_q[0][None, :], wuqt, tm=PROJ_ROWS, scale=scale)
    o = _mla_attn(qt, k, vt, gate, batch=B)
    out = _mla_out(o, h1, mla_w_out[0].astype(BF16), g_final[None, :], tm=OUT_ROWS)
    return out.reshape(B, S, D)
```

```python
import functools
import math

import jax
import jax.numpy as jnp
from jax import lax
from jax.experimental import pallas as pl
from jax.experimental.pallas import tpu as pltpu

F32 = jnp.float32
BF16 = jnp.bfloat16

SSM_HEAD_DIM = 64
SSM_GROUPS = 4
SSM_STATE = 128
MLA_HEADS = 16
QK_NOPE = 64
QK_ROPE = 32
V_HEAD = 64
ROPE_BASE = 10000.0
EPS = 1e-6
LOG2_E = math.log2(math.e)

LANES = 128
BF16_SUBLANES = 16
HEAD_LANES = 128
SSD_CHUNK = 256
CONV_HALO = 8
PROJ_ROWS = 512
IN_PROJ_ROWS = 1024
OUT_ROWS = 2048
VMEM_LIMIT = 56 * 1024 * 1024


def _params(*sem):
    return pltpu.CompilerParams(dimension_semantics=sem, vmem_limit_bytes=VMEM_LIMIT)


def _silu(x):
    return x * (1.0 / (1.0 + jnp.exp(-x)))


def _softplus(x):
    return jnp.maximum(x, 0.0) + jnp.log1p(jnp.exp(-jnp.abs(x)))


def _rms_scale(x):
    return x * lax.rsqrt(jnp.mean(x * x, axis=-1, keepdims=True) + EPS)


def _dot(a, b):
    return jnp.dot(a, b, preferred_element_type=F32)


def _nt_dot(a, b):
    return lax.dot_general(a, b, (((1,), (1,)), ((), ())), preferred_element_type=F32)


def _split2(x):
    hi = x.astype(BF16)
    lo = (x - hi.astype(F32)).astype(BF16)
    return jnp.concatenate([hi, lo], axis=-1)


def _ssm_in_proj_kernel(x_ref, g_ref, wt_ref, z_ref, xbc_ref, dt_ref):
    nz = z_ref.shape[1]
    nx = xbc_ref.shape[1]
    hn = (_rms_scale(x_ref[...]) * g_ref[...]).astype(BF16)
    z_ref[...] = _nt_dot(hn, wt_ref[:nz, :]).astype(BF16)
    xbc_ref[...] = _nt_dot(hn, wt_ref[nz:nz + nx, :]).astype(BF16)
    dt_ref[...] = _nt_dot(hn, wt_ref[nz + nx:, :])


def _ssm_in_proj(x2d, g, w, d_inner, conv_dim, tm):
    T, D = x2d.shape
    n_dt = w.shape[0] - d_inner - conv_dim
    full = lambda a: pl.BlockSpec(a.shape, lambda i: (0, 0), pipeline_mode=pl.Buffered(1))
    rows = lambda n: pl.BlockSpec((tm, n), lambda i: (i, 0))
    return pl.pallas_call(
        _ssm_in_proj_kernel,
        grid=(T // tm,),
        in_specs=[rows(D), full(g), full(w)],
        out_specs=[rows(d_inner), rows(conv_dim), rows(n_dt)],
        out_shape=[jax.ShapeDtypeStruct((T, d_inner), BF16),
                   jax.ShapeDtypeStruct((T, conv_dim), BF16),
                   jax.ShapeDtypeStruct((T, n_dt), F32)],
        compiler_params=_params("arbitrary"),
        name="ssm_in_proj",
    )(x2d, g, w)


def _ssd_kernel(z_ref, xbc_ref, dt_ref, x_ref, convw_ref, convb_ref, dtb_ref, alog_ref, dskip_ref,
                gout_ref, wout_ref, e2_ref, shift_ref, o_ref, hbuf, state, ybuf, *, d_inner, n_groups, d_state):
    Q = SSD_CHUNK
    gw = d_inner // n_groups
    heads_per_group = gw // SSM_HEAD_DIM
    pairs_per_group = gw // LANES
    kw = convw_ref.shape[0]

    @pl.when(pl.program_id(1) == 0)
    def _():
        hbuf[0:CONV_HALO, :] = jnp.zeros((CONV_HALO, hbuf.shape[1]), F32)
        state[...] = jnp.zeros(state.shape, F32)

    u_bf = xbc_ref[...]
    u = u_bf.astype(F32)
    shifted = _dot(shift_ref[...], u_bf)
    conv = convb_ref[...] + convw_ref[kw - 1:kw, :] * u
    for s in range(1, kw):
        conv = conv + convw_ref[kw - 1 - s:kw - s, :] * shifted[(s - 1) * Q:s * Q, :]
    hbuf[CONV_HALO:2 * CONV_HALO, :] = u[0:CONV_HALO, :]
    head = convb_ref[...]
    for k in range(kw):
        off = CONV_HALO - (kw - 1 - k)
        head = head + convw_ref[k:k + 1, :] * hbuf[off:off + CONV_HALO, :]
    hbuf[0:CONV_HALO, :] = u[Q - CONV_HALO:Q, :]
    conv = jnp.concatenate([head, conv[CONV_HALO:, :]], axis=0)
    xbc = _silu(conv)
    xs = xbc[:, :d_inner]

    dt = _softplus(dt_ref[...] + dtb_ref[...])
    dA = dt * (-jnp.exp(alog_ref[...]))
    row = lax.broadcasted_iota(jnp.int32, (Q, Q), 0)
    col = lax.broadcasted_iota(jnp.int32, (Q, Q), 1)
    causal = row >= col
    tril = causal.astype(BF16)
    hi = dA.astype(BF16)
    mid = (dA - hi.astype(F32)).astype(BF16)
    lo = (dA - hi.astype(F32) - mid.astype(F32)).astype(BF16)
    cum = (_dot(tril, hi) + _dot(tril, mid) + _dot(tril, lo)) * LOG2_E
    cum_t = cum.T
    cum_end = cum[Q - 1:Q, :]

    e2 = e2_ref[...]
    dt_x = _dot(_split2(dt), e2)
    ecum_x = _dot(_split2(jnp.exp2(cum)), e2)
    wend_x = _dot(_split2(jnp.exp2(cum_end - cum) * dt), e2)
    edec_x = _dot(_split2(jnp.broadcast_to(jnp.exp2(cum_end), (8, LANES))), e2)[0:1, :]

    lane = lax.broadcasted_iota(jnp.int32, (Q, LANES), 1)
    first_half = lane < SSM_HEAD_DIM
    xdt = xs * dt_x

    for g in range(n_groups):
        b_g = xbc[:, d_inner + g * d_state:d_inner + (g + 1) * d_state].astype(BF16)
        c_off = d_inner + n_groups * d_state
        c_g = xbc[:, c_off + g * d_state:c_off + (g + 1) * d_state].astype(BF16)
        cb = lax.dot_general(c_g, b_g, (((1,), (1,)), ((), ())), preferred_element_type=F32)
        gs = slice(g * gw, (g + 1) * gw)
        st = state[g]
        y_inter = _dot(c_g, st.astype(BF16)) * ecum_x[:, gs]
        xw = (xs[:, gs] * wend_x[:, gs]).astype(BF16)
        upd = lax.dot_general(b_g, xw, (((0,), (0,)), ((), ())), preferred_element_type=F32)
        state[g] = st * edec_x[:, gs] + upd
        for p in range(pairs_per_group):
            c0 = g * gw + p * LANES
            x_pair = xdt[:, c0:c0 + LANES]
            acc = y_inter[:, p * LANES:(p + 1) * LANES]
            for hh in range(2):
                h = g * heads_per_group + 2 * p + hh
                seg = cum[:, h:h + 1] - cum_t[h:h + 1, :]
                m = jnp.where(causal, cb * jnp.exp2(seg), 0.0).astype(BF16)
                keep = first_half if hh == 0 else jnp.logical_not(first_half)
                x_h = jnp.where(keep, x_pair, 0.0).astype(BF16)
                acc = acc + _dot(m, x_h)
            ybuf[:, c0:c0 + LANES] = acc

    y = (ybuf[...] + dskip_ref[...] * xs) * _silu(z_ref[...].astype(F32))
    for g in range(n_groups):
        gs = slice(g * gw, (g + 1) * gw)
        ybuf[:, gs] = _rms_scale(y[:, gs])
    yn = (ybuf[...] * gout_ref[...]).astype(BF16)
    o_ref[...] = x_ref[...] + _dot(yn, wout_ref[...])


def _ssd_mixer(z, xbc, dt_raw, x2d, convw, convb, dtb, alog, dskip, gout, wout, e2, shift, batch):
    T, d_inner = z.shape
    D = x2d.shape[1]
    conv_dim = xbc.shape[1]
    n_chunks = T // batch // SSD_CHUNK
    gw = d_inner // SSM_GROUPS
    full = lambda a: pl.BlockSpec(a.shape, lambda b, c: (0, 0))
    rows = lambda n: pl.BlockSpec((SSD_CHUNK, n), lambda b, c: (b * n_chunks + c, 0))
    kernel = functools.partial(_ssd_kernel, d_inner=d_inner, n_groups=SSM_GROUPS, d_state=SSM_STATE)
    return pl.pallas_call(
        kernel,
        grid=(batch, n_chunks),
        in_specs=[rows(d_inner), rows(conv_dim), rows(LANES), rows(D), full(convw), full(convb), full(dtb),
                  full(alog), full(dskip), full(gout), full(wout), full(e2), full(shift)],
        out_specs=rows(D),
        out_shape=jax.ShapeDtypeStruct((T, D), F32),
        scratch_shapes=[pltpu.VMEM((2 * CONV_HALO, conv_dim), F32),
                        pltpu.VMEM((SSM_GROUPS, SSM_STATE, gw), F32),
                        pltpu.VMEM((SSD_CHUNK, d_inner), F32)],
        compiler_params=_params("arbitrary", "arbitrary"),
        name="ssd_mixer",
    )(z, xbc, dt_raw, x2d, convw, convb, dtb, alog, dskip, gout, wout, e2, shift)


def _mla_proj_kernel(h_ref, pos_ref, invf_ref, gkv_ref, gq_ref, wdc_ref, wdr_ref,
                     gckv_ref, wuk_ref, wuvt_ref, wcq_ref, wgate_ref, gqn_ref, wuqt_ref,
                     qt_ref, k_ref, vt_ref, gate_ref, *, scale):
    tm = h_ref.shape[0]
    half = QK_ROPE // 2
    hs = _rms_scale(h_ref[...])
    hkv = (hs * gkv_ref[...]).astype(BF16)
    hq = (hs * gq_ref[...]).astype(BF16)

    ang = invf_ref[...] * pos_ref[...].astype(F32)
    cos = jnp.cos(ang)
    sin = jnp.sin(ang)

    ones = jnp.ones((QK_NOPE, tm), F32)
    tail = HEAD_LANES - QK_NOPE - QK_ROPE
    cos_k = jnp.concatenate([ones, cos, cos, ones[:tail]], axis=0).T
    sin_k = jnp.concatenate([0.0 * ones, sin, sin, 0.0 * ones[:tail]], axis=0).T
    ckv = (_rms_scale(_dot(hkv, wdc_ref[...])) * gckv_ref[...]).astype(BF16)
    kr2 = _dot(hkv, wdr_ref[...])
    k_rope = kr2[:, :HEAD_LANES] * cos_k + kr2[:, HEAD_LANES:] * sin_k
    vt_ref[0] = _nt_dot(wuvt_ref[...], ckv).astype(BF16)
    k_nope = _dot(ckv, wuk_ref[...])

    gate_ref[...] = _dot(hq, wgate_ref[...]).astype(BF16)
    cq = (_rms_scale(_dot(hq, wcq_ref[...])) * gqn_ref[...]).astype(BF16)
    qm = _nt_dot(wuqt_ref[...], cq)
    cos_q = cos * scale
    sin_q = sin * scale
    for h in range(k_ref.shape[1] // HEAD_LANES):
        r0 = h * HEAD_LANES
        r1 = r0 + QK_NOPE
        r2 = r1 + half
        r3 = r2 + half
        k_ref[:, r0:r0 + HEAD_LANES] = (k_nope[:, r0:r0 + HEAD_LANES] + k_rope).astype(BF16)
        x1 = qm[r1:r2, :]
        x2 = qm[r2:r3, :]
        qt_ref[0, r0:r1, :] = (qm[r0:r1, :] * scale).astype(BF16)
        qt_ref[0, r1:r2, :] = (x1 * cos_q - x2 * sin_q).astype(BF16)
        qt_ref[0, r2:r3, :] = (x1 * sin_q + x2 * cos_q).astype(BF16)
        qt_ref[0, r3:r0 + HEAD_LANES, :] = qm[r3:r0 + HEAD_LANES, :].astype(BF16)


def _mla_proj(h1, pos, invf, gkv, gq, wdc, wdr, gckv, wuk, wuvt, wcq, wgate, gqn, wuqt, tm, scale):
    T, D = h1.shape
    full = lambda a: pl.BlockSpec(a.shape, lambda i: (0, 0))
    rows = lambda n: pl.BlockSpec((tm, n), lambda i: (i, 0))
    cols = lambda n: pl.BlockSpec((1, n, tm), lambda i: (i, 0, 0))
    hq = wuqt.shape[0]
    hv = wuvt.shape[0]
    hg = wgate.shape[1]
    weights = (invf, gkv, gq, wdc, wdr, gckv, wuk, wuvt, wcq, wgate, gqn, wuqt)
    return pl.pallas_call(
        functools.partial(_mla_proj_kernel, scale=scale),
        grid=(T // tm,),
        in_specs=[rows(D), pl.BlockSpec((1, tm), lambda i: (0, i))] + [full(w) for w in weights],
        out_specs=[cols(hq), rows(hq), cols(hv), rows(hg)],
        out_shape=[jax.ShapeDtypeStruct((T // tm, hq, tm), BF16), jax.ShapeDtypeStruct((T, hq), BF16),
                   jax.ShapeDtypeStruct((T // tm, hv, tm), BF16), jax.ShapeDtypeStruct((T, hg), BF16)],
        compiler_params=_params("arbitrary"),
        name="mla_proj",
    )(h1, pos.reshape(1, T), *weights)


ATTN_TILE = 1024
ATTN_UNIT_COLS = 256


def _attn_kernel(qt_ref, k_ref, vt_ref, gate_ref, o_ref, *scratch, tm):
    tq = ATTN_TILE
    qc = ATTN_UNIT_COLS
    n_slices = tq // qc
    n_units = 2 * n_slices
    sub = tq // tm
    s_refs = scratch[:n_units]
    p_refs = scratch[n_units:2 * n_units]
    mx_ref, m_ref, alpha_ref, l_ref, acc_ref = scratch[2 * n_units:]
    qi = pl.program_id(2)

    m_ref[...] = jnp.full(m_ref.shape, -jnp.inf, F32)
    l_ref[...] = jnp.zeros(l_ref.shape, F32)
    acc_ref[...] = jnp.zeros(acc_ref.shape, F32)

    def head_slice(u):
        h, r = divmod(u, n_slices)
        return h, n_slices - 1 - r

    def scores(u, j):
        h, c = head_slice(u)
        hs = slice(h * HEAD_LANES, (h + 1) * HEAD_LANES)
        ks = pl.ds(pl.multiple_of(j * tq, tq), tq)
        t, off = divmod(c * qc, tm)
        s = _dot(k_ref[ks, hs], qt_ref[t, hs, off:off + qc])
        s_refs[u][...] = s
        mx_ref[u] = jnp.max(s, axis=0, keepdims=True)

    def diag_keys(u):
        return (head_slice(u)[1] + 1) * qc

    def probs(u, masked):
        c = head_slice(u)[1]
        nk = diag_keys(u) if masked else tq
        s = s_refs[u][0:nk, :]
        if masked:
            key = lax.broadcasted_iota(jnp.int32, (nk, qc), 0)
            qry = lax.broadcasted_iota(jnp.int32, (nk, qc), 1) + c * qc
            s = jnp.where(key <= qry, s, -jnp.inf)
            mx = jnp.max(s, axis=0, keepdims=True)
        else:
            mx = mx_ref[u]
        m_old = m_ref[u]
        m_new = jnp.maximum(m_old, mx)
        alpha_ref[u] = jnp.exp2(m_old - m_new)
        m_ref[u] = m_new
        p_refs[u][0:nk, :] = jnp.exp2(s - m_new).astype(BF16)

    def values(u, j, masked=False):
        h = u // n_slices
        vs = slice(h * V_HEAD, (h + 1) * V_HEAD)
        nk = diag_keys(u) if masked else tq
        pv = None
        for t in range(sub):
            n = min(nk - t * tm, tm)
            if n <= 0:
                break
            lhs = jnp.concatenate([vt_ref[j * sub + t, vs, 0:n], jnp.ones((BF16_SUBLANES, n), BF16)], axis=0)
            part = _dot(lhs, p_refs[u][t * tm:t * tm + n, :])
            pv = part if pv is None else pv + part
        alpha = alpha_ref[u]
        acc_ref[u] = alpha * acc_ref[u] + pv[:V_HEAD, :]
        l_ref[u] = alpha * l_ref[u] + pv[V_HEAD:V_HEAD + 1, :]

    for u in range(n_units):
        scores(u, 0)

    def body(j, _):
        for u in range(n_units):
            probs(u, masked=False)
            scores(u, j + 1)
            if u > 0:
                values(u - 1, j)
        values(n_units - 1, j)
        return 0

    lax.fori_loop(0, qi, body, 0)
    for u in range(n_units):
        probs(u, masked=True)
        if u > 0:
            values(u - 1, qi, masked=True)
    values(n_units - 1, qi, masked=True)

    o_t = jnp.concatenate(
        [jnp.concatenate([acc_ref[u] * (1.0 / l_ref[u]) for u in reversed(range(h * n_slices, (h + 1) * n_slices))],
                         axis=1) for h in range(2)], axis=0)
    o_ref[...] = (o_t.T * _silu(gate_ref[...].astype(F32))).astype(BF16)


def _mla_attn(qt, k, vt, gate, batch):
    T = k.shape[0]
    tm = qt.shape[2]
    tq = ATTN_TILE
    qc = ATTN_UNIT_COLS
    S = T // batch
    n_q = S // tq
    sub = tq // tm
    n_pairs = vt.shape[1] // LANES
    n_units = 2 * (tq // qc)
    stat = pltpu.VMEM((n_units, 1, qc), F32)
    return pl.pallas_call(
        functools.partial(_attn_kernel, tm=tm),
        grid=(batch, n_pairs, n_q),
        in_specs=[pl.BlockSpec((sub, 2 * HEAD_LANES, tm), lambda b, p, i: (b * n_q + i, p, 0)),
                  pl.BlockSpec((S, 2 * HEAD_LANES), lambda b, p, i: (b, p)),
                  pl.BlockSpec((S // tm, LANES, tm), lambda b, p, i: (b, p, 0)),
                  pl.BlockSpec((tq, LANES), lambda b, p, i: (b * n_q + i, p))],
        out_specs=pl.BlockSpec((tq, LANES), lambda b, p, i: (b * n_q + i, p)),
        out_shape=jax.ShapeDtypeStruct((T, vt.shape[1]), BF16),
        scratch_shapes=([pltpu.VMEM((tq, qc), F32)] * n_units + [pltpu.VMEM((tq, qc), BF16)] * n_units
                        + [stat, stat, stat, stat, pltpu.VMEM((n_units, V_HEAD, qc), F32)]),
        compiler_params=_params("arbitrary", "arbitrary", "arbitrary"),
        name="mla_attn",
    )(qt, k, vt, gate)


def _mla_out_kernel(o_ref, h_ref, w_ref, g_ref, out_ref):
    h = h_ref[...] + _dot(o_ref[...], w_ref[...])
    out_ref[...] = _rms_scale(h) * g_ref[...]


def _mla_out(o, h1, w, g, tm):
    T, D = h1.shape
    full = lambda a: pl.BlockSpec(a.shape, lambda i: (0, 0), pipeline_mode=pl.Buffered(1))
    rows = lambda n: pl.BlockSpec((tm, n), lambda i: (i, 0))
    return pl.pallas_call(
        _mla_out_kernel,
        grid=(T // tm,),
        in_specs=[rows(o.shape[1]), rows(D), full(w), full(g)],
        out_specs=rows(D),
        out_shape=jax.ShapeDtypeStruct((T, D), F32),
        compiler_params=_params("arbitrary"),
        name="mla_out",
    )(o, h1, w, g)


def _pad_cols(a, n):
    return jnp.pad(a, ((0, 0), (0, n - a.shape[1])))


def _head_slots(w_nope, w_rope):
    k, h = w_nope.shape[0], w_nope.shape[1]
    pad = jnp.zeros((k, h, HEAD_LANES - QK_NOPE - QK_ROPE), w_nope.dtype)
    return jnp.concatenate([w_nope, w_rope, pad], axis=-1).reshape(k, h * HEAD_LANES)


def _rotate_half_cols(w_rope):
    w1, w2 = jnp.split(w_rope, 2, axis=-1)
    return jnp.concatenate([-w2, w1], axis=-1)


def kernel(x, positions, g_pre, ssm_w_in, ssm_conv_w, ssm_conv_b, ssm_dt_bias, ssm_A_log, ssm_D, ssm_g_out,
           ssm_w_out, kv_g_in, kv_w_down, kv_g_latent, kv_w_up, mla_w_in, mla_g_q, mla_w_uq, mla_w_out, g_final):
    B, S, D = x.shape
    T = B * S
    d_inner = ssm_w_out.shape[1]
    n_heads = ssm_dt_bias.shape[1]
    conv_dim = ssm_conv_w.shape[2]
    kv_lora = kv_g_latent.shape[0]
    q_lora = mla_g_q.shape[1]
    H = MLA_HEADS
    assert d_inner == n_heads * SSM_HEAD_DIM and n_heads <= LANES
    assert S % SSD_CHUNK == 0 and ssm_conv_w.shape[1] - 1 <= CONV_HALO
    assert S % ATTN_TILE == 0 and ATTN_TILE % PROJ_ROWS == 0 and T % OUT_ROWS == 0
    assert mla_w_out.shape[1] == H * V_HEAD and H % 2 == 0
    x2d = x.reshape(T, D)

    w_in = jnp.pad(ssm_w_in[0].T, ((0, LANES - n_heads), (0, 0))).astype(BF16)
    z, xbc, dt_raw = _ssm_in_proj(x2d, g_pre[0][None, :], w_in, d_inner, conv_dim, tm=IN_PROJ_ROWS)

    head_of_channel = jnp.arange(d_inner) // SSM_HEAD_DIM
    expand = (jnp.arange(LANES)[:, None] == head_of_channel[None, :]).astype(BF16)
    e2 = jnp.concatenate([expand, expand], axis=0)
    t_idx = jnp.arange(SSD_CHUNK)
    kw = ssm_conv_w.shape[1]
    shift = jnp.concatenate([(t_idx[:, None] - s == t_idx[None, :]) for s in range(1, kw)], axis=0).astype(BF16)
    h1 = _ssd_mixer(
        z, xbc, dt_raw, x2d, ssm_conv_w[0], ssm_conv_b[0][None, :],
        _pad_cols(ssm_dt_bias[0][None, :], LANES), _pad_cols(ssm_A_log[0][None, :], LANES),
        jnp.repeat(ssm_D[0], SSM_HEAD_DIM)[None, :], ssm_g_out[0][None, :], ssm_w_out[0].astype(BF16), e2, shift,
        batch=B)

    invf = (ROPE_BASE ** (-jnp.arange(0, QK_ROPE, 2, dtype=F32) / QK_ROPE))[:, None]
    wdc = kv_w_down[:, :kv_lora].astype(BF16)
    wr = kv_w_down[:, kv_lora:]
    slot1 = lambda w: jnp.pad(w, ((0, 0), (QK_NOPE, HEAD_LANES - QK_NOPE - QK_ROPE)))
    wdr = jnp.concatenate([slot1(wr), slot1(_rotate_half_cols(wr))], axis=1).astype(BF16)
    wup = kv_w_up.reshape(kv_lora, H, QK_NOPE + V_HEAD)
    wuk = _head_slots(wup[:, :, :QK_NOPE], jnp.zeros((kv_lora, H, QK_ROPE), F32)).astype(BF16)
    wuvt = wup[:, :, QK_NOPE:].reshape(kv_lora, H * V_HEAD).T.astype(BF16)
    w_in_b = mla_w_in[0]
    wcq = w_in_b[:, :q_lora].astype(BF16)
    wgate = w_in_b[:, q_lora:].astype(BF16)
    wq = mla_w_uq[0].reshape(q_lora, H, QK_NOPE + QK_ROPE)
    wq_nope, wq_rope = wq[:, :, :QK_NOPE], wq[:, :, QK_NOPE:]
    wuqt = _head_slots(wq_nope, wq_rope).T.astype(BF16)
    scale = float((QK_NOPE + QK_ROPE) ** -0.5 * math.log2(math.e))

    qt, k, vt, gate = _mla_proj(
        h1, positions, invf, kv_g_in[None, :], g_pre[1][None, :], wdc, wdr, kv_g_latent[None, :],
        wuk, wuvt, wcq, wgate, mla_g_q[0][None, :], wuqt, tm=PROJ_ROWS, scale=scale)
    o = _mla_attn(qt, k, vt, gate, batch=B)
    out = _mla_out(o, h1, mla_w_out[0].astype(BF16), g_final[None, :], tm=OUT_ROWS)
    return out.reshape(B, S, D)
```
